```python
import math
import jax, jax.numpy as jnp
from jax import lax
import numpy as np

D_MODEL = 1024
BATCH = 8
SEQ = 4096
DEPTH = 2

N_MIXERS = 2
EPS = 1e-5
CONV_WIDTH = 31
CONF_INNER = D_MODEL
SSM_EXPAND = 2
D_INNER = SSM_EXPAND * D_MODEL
SSM_HEAD_DIM = 64
SSM_HEADS = D_INNER // SSM_HEAD_DIM
SSM_GROUPS = 8
SSM_STATE = 128
SSM_CONV = 4
SSD_CHUNK = 128
SSM_CONV_DIM = D_INNER + 2 * SSM_GROUPS * SSM_STATE
SSM_IN_WIDTH = 2 * D_INNER + 2 * SSM_GROUPS * SSM_STATE + SSM_HEADS
N_EXPERTS = 32
TOP_K = 4
D_FF = D_MODEL
SWIGLU_LIMIT = 7.0
SWIGLU_ALPHA = 1.702
MOE_BLOCK = 256

kernel_name = 'hybrid_conformer_mamba2_moe_adaln'


def rmsnorm(x, g):
    xf = x.astype(jnp.float32)
    y = xf * lax.rsqrt(jnp.mean(xf * xf, axis=-1, keepdims=True) + EPS) * g.astype(jnp.float32)
    return y.astype(x.dtype)


def layernorm(x, g, b):
    xf = x.astype(jnp.float32)
    mu = jnp.mean(xf, axis=-1, keepdims=True)
    var = jnp.mean(jnp.square(xf - mu), axis=-1, keepdims=True)
    y = (xf - mu) * lax.rsqrt(var + EPS) * g.astype(jnp.float32) + b.astype(jnp.float32)
    return y.astype(x.dtype)


def modulate(h, shift, scale):
    return h * (1 + scale[:, None, :]) + shift[:, None, :]


def causal_depthwise_conv(u, w, b):
    width, ch = w.shape
    out = lax.conv_general_dilated(
        u, w[:, None, :].astype(u.dtype), window_strides=(1,), padding=[(width - 1, 0)],
        dimension_numbers=('NWC', 'WIO', 'NWC'), feature_group_count=ch)
    return out + b


def conformer_conv_module(h, pw1_w, pw1_b, dw_w, dw_b, ln_g, ln_b, pw2_w, pw2_b):
    u = h @ pw1_w + pw1_b
    a, gt = jnp.split(u, 2, axis=-1)
    u = a * jax.nn.sigmoid(gt)
    u = causal_depthwise_conv(u, dw_w, dw_b)
    u = layernorm(u, ln_g, ln_b)
    u = jax.nn.silu(u)
    return u @ pw2_w + pw2_b


def ssd_chunked_scan(x, dt, A, Bm, Cm):
    bsz, s, nh, p = x.shape
    g, n = Bm.shape[-2:]
    hg = nh // g
    L = SSD_CHUNK
    nc = s // L

    def chunks(t):
        return jnp.moveaxis(t.reshape((bsz, nc, L) + t.shape[2:]), 1, 0)

    xc = chunks(x.reshape(bsz, s, g, hg, p))
    dtc = chunks(dt.reshape(bsz, s, g, hg))
    ac = dtc * A.reshape(g, hg)
    bc = chunks(Bm)
    cc = chunks(Cm)
    causal = jnp.tril(jnp.ones((L, L), dtype=bool))

    def step(state, inp):
        xk, dtk, ak, bk, ck = inp
        cum = jnp.cumsum(ak, axis=1)
        seg = cum[:, :, None] - cum[:, None, :]
        decay = jnp.exp(jnp.where(causal[None, :, :, None, None], seg, -jnp.inf))
        cb = jnp.einsum('blgn,bsgn->blsg', ck, bk)
        w = cb[..., None] * decay * dtk[:, None]
        y_diag = jnp.einsum('blsgh,bsghp->blghp', w, xk)
        y_off = jnp.einsum('blgn,bghpn->blghp', ck, state) * jnp.exp(cum)[..., None]
        to_end = jnp.exp(cum[:, -1:] - cum) * dtk
        new_state = (state * jnp.exp(cum[:, -1])[..., None, None]
                     + jnp.einsum('bsgn,bsgh,bsghp->bghpn', bk, to_end, xk))
        return new_state, y_diag + y_off

    state0 = jnp.zeros((bsz, g, hg, p, n), jnp.float32)
    _, ys = lax.scan(step, state0, (xc, dtc, ac, bc, cc))
    return jnp.moveaxis(ys, 0, 1).reshape(bsz, s, nh, p)


def mamba2_mixer(h, in_w, conv_w, conv_b, dt_bias, a_log, d_skip, norm_g, out_w):
    bsz, s, _ = h.shape
    zxbcdt = h @ in_w
    z = zxbcdt[..., :D_INNER]
    xbc = zxbcdt[..., D_INNER:D_INNER + SSM_CONV_DIM]
    dt = zxbcdt[..., D_INNER + SSM_CONV_DIM:]
    xbc = jax.nn.silu(causal_depthwise_conv(xbc, conv_w, conv_b))
    xs = xbc[..., :D_INNER].astype(jnp.float32).reshape(bsz, s, SSM_HEADS, SSM_HEAD_DIM)
    bm = xbc[..., D_INNER:D_INNER + SSM_GROUPS * SSM_STATE].astype(jnp.float32).reshape(bsz, s, SSM_GROUPS, SSM_STATE)
    cm = xbc[..., D_INNER + SSM_GROUPS * SSM_STATE:].astype(jnp.float32).reshape(bsz, s, SSM_GROUPS, SSM_STATE)
    dt = jax.nn.softplus(dt.astype(jnp.float32) + dt_bias.astype(jnp.float32))
    A = -jnp.exp(a_log.astype(jnp.float32))
    y = ssd_chunked_scan(xs, dt, A, bm, cm)
    y = y + d_skip.astype(jnp.float32)[:, None] * xs
    y = y.reshape(bsz, s, D_INNER) * jax.nn.silu(z.astype(jnp.float32))
    yg = y.reshape(bsz, s, SSM_GROUPS, D_INNER // SSM_GROUPS)
    yg = yg * lax.rsqrt(jnp.mean(yg * yg, axis=-1, keepdims=True) + EPS)
    y = yg.reshape(bsz, s, D_INNER) * norm_g.astype(jnp.float32)
    return y.astype(h.dtype) @ out_w


def moe_ffn(h, router_w, router_b, w1, b1, w2, b2):
    bsz, s, d = h.shape
    t = bsz * s
    hf = h.reshape(t, d)
    logits = (hf @ router_w + router_b).astype(jnp.float32)
    top_val, top_idx = lax.top_k(logits, TOP_K)
    gate = jax.nn.softmax(top_val, axis=-1)
    n_assign = t * TOP_K
    flat_e = top_idx.reshape(-1)
    flat_tok = jnp.arange(n_assign, dtype=jnp.int32) // TOP_K
    order = jnp.argsort(flat_e)
    sorted_e = flat_e[order]
    counts = jnp.bincount(flat_e, length=N_EXPERTS)
    padded = ((counts + MOE_BLOCK - 1) // MOE_BLOCK) * MOE_BLOCK
    start = jnp.cumsum(counts) - counts
    ends = jnp.cumsum(padded)
    pstart = ends - padded
    dest = pstart[sorted_e] + (jnp.arange(n_assign, dtype=jnp.int32) - start[sorted_e])
    n_slots = n_assign + N_EXPERTS * MOE_BLOCK
    n_blocks = n_slots // MOE_BLOCK
    slot_tok = jnp.full((n_slots,), t, dtype=jnp.int32).at[dest].set(flat_tok[order])
    slot_gate = jnp.zeros((n_slots,), jnp.float32).at[dest].set(gate.reshape(-1)[order])
    block_e = jnp.minimum(
        jnp.searchsorted(ends, jnp.arange(n_blocks, dtype=jnp.int32) * MOE_BLOCK, side='right'),
        N_EXPERTS - 1)
    xb = jnp.take(hf, slot_tok, axis=0, mode='fill', fill_value=0).reshape(n_blocks, MOE_BLOCK, d)

    def expert_block(args):
        xk, e = args
        gu = xk @ w1[e] + b1[e]
        g_lin, u_lin = gu[:, :D_FF], gu[:, D_FF:]
        g_lin = jnp.minimum(g_lin, SWIGLU_LIMIT)
        u_lin = jnp.clip(u_lin, -SWIGLU_LIMIT, SWIGLU_LIMIT)
        act = (u_lin + 1) * (g_lin * jax.nn.sigmoid(SWIGLU_ALPHA * g_lin))
        return act @ w2[e] + b2[e]

    ys = lax.map(expert_block, (xb, block_e)).reshape(n_slots, d)
    out = jnp.zeros((t, d), ys.dtype).at[slot_tok].add(ys * slot_gate[:, None].astype(ys.dtype), mode='drop')
    return out.reshape(bsz, s, d)


def setup_inputs(seed: int = 0) -> dict:
    key = jax.random.key(seed)
    ks = jax.random.split(key, 32)
    f32 = jnp.float32
    D = D_MODEL
    n_a = (DEPTH + N_MIXERS - 1) // N_MIXERS
    n_b = DEPTH // N_MIXERS

    def nrm(k, shape, scale):
        return jax.random.normal(k, shape, f32) * scale

    dt_u = jax.random.uniform(ks[18], (n_b, SSM_HEADS), f32)
    dt0 = jnp.exp(dt_u * (math.log(0.1) - math.log(0.001)) + math.log(0.001))
    return {
        'x': nrm(ks[0], (BATCH, SEQ, D), 1.0),
        'c': nrm(ks[1], (BATCH, D), 1.0),
        'ada_w': nrm(ks[2], (DEPTH, D, 6 * D), 0.02),
        'ada_b': nrm(ks[3], (DEPTH, 6 * D), 0.02),
        'mix_norm_g': 1.0 + nrm(ks[4], (DEPTH, D), 0.02),
        'ffn_norm_g': 1.0 + nrm(ks[5], (DEPTH, D), 0.02),
        'conf_pw1_w': nrm(ks[6], (n_a, D, 2 * CONF_INNER), D ** -0.5),
        'conf_pw1_b': nrm(ks[7], (n_a, 2 * CONF_INNER), 0.02),
        'conf_dw_w': nrm(ks[8], (n_a, CONV_WIDTH, CONF_INNER), CONV_WIDTH ** -0.5),
        'conf_dw_b': nrm(ks[9], (n_a, CONF_INNER), 0.02),
        'conf_ln_g': 1.0 + nrm(ks[10], (n_a, CONF_INNER), 0.02),
        'conf_ln_b': nrm(ks[11], (n_a, CONF_INNER), 0.02),
        'conf_pw2_w': nrm(ks[12], (n_a, CONF_INNER, D), CONF_INNER ** -0.5),
        'conf_pw2_b': nrm(ks[13], (n_a, D), 0.02),
        'ssm_in_w': nrm(ks[14], (n_b, D, SSM_IN_WIDTH), D ** -0.5),
        'ssm_conv_w': nrm(ks[15], (n_b, SSM_CONV, SSM_CONV_DIM), SSM_CONV ** -0.5),
        'ssm_conv_b': nrm(ks[16], (n_b, SSM_CONV_DIM), 0.02),
        'ssm_dt_bias': dt0 + jnp.log(-jnp.expm1(-dt0)),
        'ssm_a_log': jnp.log(jax.random.uniform(ks[17], (n_b, SSM_HEADS), f32, minval=1.0, maxval=16.0)),
        'ssm_d': 1.0 + nrm(ks[19], (n_b, SSM_HEADS), 0.1),
        'ssm_norm_g': 1.0 + nrm(ks[20], (n_b, D_INNER), 0.02),
        'ssm_out_w': nrm(ks[21], (n_b, D_INNER, D), D_INNER ** -0.5),
        'router_w': nrm(ks[22], (DEPTH, D, N_EXPERTS), D ** -0.5),
        'router_b': nrm(ks[23], (DEPTH, N_EXPERTS), 0.01),
        'exp_w1': nrm(ks[24], (DEPTH, N_EXPERTS, D, 2 * D_FF), D ** -0.5),
        'exp_b1': nrm(ks[25], (DEPTH, N_EXPERTS, 2 * D_FF), 0.02),
        'exp_w2': nrm(ks[26], (DEPTH, N_EXPERTS, D_FF, D), D_FF ** -0.5),
        'exp_b2': nrm(ks[27], (DEPTH, N_EXPERTS, D), 0.02),
        'final_norm_g': 1.0 + nrm(ks[28], (D,), 0.02),
    }


def reference(x, c, ada_w, ada_b, mix_norm_g, ffn_norm_g,
              conf_pw1_w, conf_pw1_b, conf_dw_w, conf_dw_b, conf_ln_g, conf_ln_b, conf_pw2_w, conf_pw2_b,
              ssm_in_w, ssm_conv_w, ssm_conv_b, ssm_dt_bias, ssm_a_log, ssm_d, ssm_norm_g, ssm_out_w,
              router_w, router_b, exp_w1, exp_b1, exp_w2, exp_b2, final_norm_g):
    c_act = jax.nn.silu(c)
    for i in range(DEPTH):
        mod = c_act @ ada_w[i] + ada_b[i]
        sh1, sc1, g1, sh2, sc2, g2 = jnp.split(mod, 6, axis=-1)
        h = modulate(rmsnorm(x, mix_norm_g[i]), sh1, sc1)
        j = i // N_MIXERS
        if i % N_MIXERS == 0:
            y = conformer_conv_module(h, conf_pw1_w[j], conf_pw1_b[j], conf_dw_w[j], conf_dw_b[j],
                                      conf_ln_g[j], conf_ln_b[j], conf_pw2_w[j], conf_pw2_b[j])
        else:
            y = mamba2_mixer(h, ssm_in_w[j], ssm_conv_w[j], ssm_conv_b[j], ssm_dt_bias[j],
                             ssm_a_log[j], ssm_d[j], ssm_norm_g[j], ssm_out_w[j])
        x = x + g1[:, None, :] * y
        h = modulate(rmsnorm(x, ffn_norm_g[i]), sh2, sc2)
        x = x + g2[:, None, :] * moe_ffn(h, router_w[i], router_b[i], exp_w1[i], exp_b1[i], exp_w2[i], exp_b2[i])
    return rmsnorm(x, final_norm_g)
```

```python
import functools

import jax
import jax.numpy as jnp
from jax import lax
from jax.experimental import pallas as pl
from jax.experimental.pallas import tpu as pltpu

F32 = jnp.float32
BF16 = jnp.bfloat16
I32 = jnp.int32

EPS = 1e-5
CONV_WIDTH = 31
SSM_HEAD_DIM = 64
SSM_GROUPS = 8
SSM_STATE = 128
SSM_CONV = 4
SSD_CHUNK = 128
N_EXPERTS = 32
TOP_K = 4
SWIGLU_LIMIT = 7.0
SWIGLU_ALPHA = 1.702

LANES = 128
SUBLANES = 8
SLOT_BLOCK = 256
TOKEN_TILE = 256
SEQ_TILE = 256
HALO = 32
SSM_HALO = 8
VMEM_LIMIT = 56 * 1024 * 1024

HIGHEST = lax.Precision.HIGHEST


def _params(n_axes, vmem=VMEM_LIMIT):
    return pltpu.CompilerParams(dimension_semantics=("arbitrary",) * n_axes, vmem_limit_bytes=vmem)


def _sigmoid(v):
    return 1.0 / (1.0 + jnp.exp(-v))


def _silu(v):
    return v * _sigmoid(v)


def _rms_modulate(x, g, shift, scale):
    ms = jnp.mean(x * x, axis=-1, keepdims=True)
    return (x * lax.rsqrt(ms + EPS) * g) * (1.0 + scale) + shift


def _split_bf16(v):
    hi = v.astype(BF16)
    lo = (v - hi.astype(F32)).astype(BF16)
    return hi, lo


def _ada_kernel(c_ref, w_ref, b_ref, o_ref):
    c = c_ref[...]
    o_ref[0] = jnp.dot(_silu(c), w_ref[0], precision=HIGHEST, preferred_element_type=F32) + b_ref[0]


def _ada(c, ada_w, ada_b):
    depth, d, n = ada_w.shape
    bsz = c.shape[0]
    tn = 1536
    return pl.pallas_call(
        _ada_kernel,
        grid=(depth, n // tn),
        in_specs=[pl.BlockSpec((bsz, d), lambda l, j: (0, 0)),
                  pl.BlockSpec((1, d, tn), lambda l, j: (l, 0, j)),
                  pl.BlockSpec((1, 1, tn), lambda l, j: (l, 0, j))],
        out_specs=pl.BlockSpec((1, bsz, tn), lambda l, j: (l, 0, j)),
        out_shape=jax.ShapeDtypeStruct((depth, bsz, n), F32),
        compiler_params=_params(2),
        name="ada",
    )(c, ada_w, ada_b.reshape(depth, 1, n))


def _conformer_kernel(x_ref, sh_ref, sc_ref, gate_ref, ng_ref, w1_ref, b1_ref, dw_ref, dwb_ref,
                      lng_ref, lnb_ref, w2_ref, b2_ref, o_ref, ubuf, cbuf):
    ts, d = x_ref.shape[1], x_ref.shape[2]
    x = x_ref[0]
    h = _rms_modulate(x, ng_ref[...], sh_ref[0], sc_ref[0])
    u = jnp.dot(h.astype(BF16), w1_ref[...], preferred_element_type=F32) + b1_ref[...]
    u = u[:, :d] * _sigmoid(u[:, d:])

    @pl.when(pl.program_id(1) == 0)
    def _():
        ubuf[0:HALO, :] = jnp.zeros((HALO, d), F32)

    ubuf[HALO:HALO + ts, :] = u
    base = HALO - (CONV_WIDTH - 1)

    def lane_chunk(c, carry):
        lo = pl.multiple_of(c * LANES, LANES)
        acc = jnp.zeros((ts, LANES), F32) + dwb_ref[:, pl.ds(lo, LANES)]
        for k in range(CONV_WIDTH):
            acc = acc + dw_ref[k:k + 1, pl.ds(lo, LANES)] * ubuf[base + k:base + k + ts, pl.ds(lo, LANES)]
        cbuf[:, pl.ds(lo, LANES)] = acc
        return carry

    lax.fori_loop(0, d // LANES, lane_chunk, 0)
    ubuf[0:HALO, :] = ubuf[ts:ts + HALO, :]

    v = cbuf[...]
    mu = jnp.mean(v, axis=-1, keepdims=True)
    vc = v - mu
    var = jnp.mean(vc * vc, axis=-1, keepdims=True)
    v = _silu(vc * lax.rsqrt(var + EPS) * lng_ref[...] + lnb_ref[...])
    y = jnp.dot(v.astype(BF16), w2_ref[...], preferred_element_type=F32) + b2_ref[...]
    o_ref[0] = x + gate_ref[0] * y


def _conformer(x, shift, scale, gate, norm_g, pw1_w, pw1_b, dw_w, dw_b, ln_g, ln_b, pw2_w, pw2_b):
    bsz, s, d = x.shape
    ts = SEQ_TILE
    row = lambda a: a.reshape(1, -1)
    per_batch = pl.BlockSpec((1, 1, d), lambda b, j: (b, 0, 0))
    full = lambda a: pl.BlockSpec(a.shape, lambda b, j: (0,) * a.ndim)
    consts = [row(norm_g), pw1_w.astype(BF16), row(pw1_b), dw_w, row(dw_b), row(ln_g), row(ln_b),
              pw2_w.astype(BF16), row(pw2_b)]
    return pl.pallas_call(
        _conformer_kernel,
        grid=(bsz, s // ts),
        in_specs=[pl.BlockSpec((1, ts, d), lambda b, j: (b, j, 0)), per_batch, per_batch, per_batch]
                 + [full(a) for a in consts],
        out_specs=pl.BlockSpec((1, ts, d), lambda b, j: (b, j, 0)),
        out_shape=jax.ShapeDtypeStruct((bsz, s, d), F32),
        scratch_shapes=[pltpu.VMEM((HALO + ts, d), F32), pltpu.VMEM((ts, d), F32)],
        compiler_params=_params(2),
        name="conformer",
    )(x, shift, scale, gate, *consts)


def _ssm_in_kernel(x_ref, sh_ref, sc_ref, ng_ref, wz_ref, wx_ref, wdt_ref, cw_ref, cb_ref, dtb_ref,
                   z_ref, xs_ref, b_ref, c_ref, dt_ref, buf):
    ts = x_ref.shape[1]
    d_inner = xs_ref.shape[2]
    gn = b_ref.shape[2]
    conv_dim = buf.shape[1]
    nh = d_inner // SSM_HEAD_DIM
    h = _rms_modulate(x_ref[0], ng_ref[...], sh_ref[0], sc_ref[0]).astype(BF16)
    z_ref[0] = jnp.dot(h, wz_ref[...], preferred_element_type=F32).astype(BF16)
    dt = jnp.dot(h, wdt_ref[...], preferred_element_type=F32) + dtb_ref[...]
    dt = jnp.maximum(dt, 0.0) + jnp.log(1.0 + jnp.exp(-jnp.abs(dt)))
    dt_ref[0] = jnp.where(lax.broadcasted_iota(I32, dt.shape, 1) < nh, dt, 0.0)

    @pl.when(pl.program_id(1) == 0)
    def _():
        buf[0:SSM_HALO, :] = jnp.zeros((SSM_HALO, conv_dim), F32)

    buf[SSM_HALO:SSM_HALO + ts, :] = jnp.dot(h, wx_ref[...], preferred_element_type=F32)
    base = SSM_HALO - (SSM_CONV - 1)
    cw = 512

    def lane_chunk(c, carry):
        lo = pl.multiple_of(c * cw, cw)
        acc = jnp.zeros((ts, cw), F32) + cb_ref[:, pl.ds(lo, cw)]
        for k in range(SSM_CONV):
            acc = acc + cw_ref[k:k + 1, pl.ds(lo, cw)] * buf[base + k:base + k + ts, pl.ds(lo, cw)]
        acc = _silu(acc).astype(BF16)

        @pl.when(lo < d_inner)
        def _():
            xs_ref[0, :, pl.ds(pl.multiple_of(lo, cw), cw)] = acc

        @pl.when((lo >= d_inner) & (lo < d_inner + gn))
        def _():
            b_ref[0, :, pl.ds(pl.multiple_of(lo - d_inner, cw), cw)] = acc

        @pl.when(lo >= d_inner + gn)
        def _():
            c_ref[0, :, pl.ds(pl.multiple_of(lo - d_inner - gn, cw), cw)] = acc

        return carry

    lax.fori_loop(0, conv_dim // cw, lane_chunk, 0)
    buf[0:SSM_HALO, :] = buf[ts:ts + SSM_HALO, :]


def _ssm_in(x, shift, scale, norm_g, in_w, conv_w, conv_b, dt_bias):
    bsz, s, d = x.shape
    ts = SEQ_TILE
    nh = dt_bias.shape[0]
    d_inner = nh * SSM_HEAD_DIM
    gn = SSM_GROUPS * SSM_STATE
    conv_dim = d_inner + 2 * gn
    w = in_w.astype(BF16)
    lane_pad = lambda a: jnp.pad(a, ((0, 0), (0, LANES - nh)))
    consts = [norm_g.reshape(1, d), w[:, :d_inner], w[:, d_inner:d_inner + conv_dim],
              lane_pad(w[:, d_inner + conv_dim:]), conv_w, conv_b.reshape(1, conv_dim),
              lane_pad(dt_bias.reshape(1, nh))]
    per_batch = pl.BlockSpec((1, 1, d), lambda b, j: (b, 0, 0))
    full = lambda a: pl.BlockSpec(a.shape, lambda b, j: (0,) * a.ndim)
    tile = lambda n: pl.BlockSpec((1, ts, n), lambda b, j: (b, j, 0))
    return pl.pallas_call(
        _ssm_in_kernel,
        grid=(bsz, s // ts),
        in_specs=[tile(d), per_batch, per_batch] + [full(a) for a in consts],
        out_specs=[tile(d_inner), tile(d_inner), tile(gn), tile(gn), tile(LANES)],
        out_shape=[jax.ShapeDtypeStruct((bsz, s, d_inner), BF16), jax.ShapeDtypeStruct((bsz, s, d_inner), BF16),
                   jax.ShapeDtypeStruct((bsz, s, gn), BF16), jax.ShapeDtypeStruct((bsz, s, gn), BF16),
                   jax.ShapeDtypeStruct((bsz, s, LANES), F32)],
        scratch_shapes=[pltpu.VMEM((SSM_HALO + ts, conv_dim), F32)],
        compiler_params=_params(2),
        name="ssm_in",
    )(x, shift, scale, *consts)


def _expand_heads(v, rep):
    hi, lo = _split_bf16(v)
    return (jnp.dot(hi, rep, preferred_element_type=F32) + jnp.dot(lo, rep, preferred_element_type=F32))


def _ssd_kernel(xs_ref, b_ref, c_ref, dt_ref, z_ref, alog_ref, dskip_ref, ng_ref, o_ref, state):
    L = xs_ref.shape[1]
    d_inner = xs_ref.shape[2]
    hp = dt_ref.shape[2]
    hg = d_inner // SSM_HEAD_DIM // SSM_GROUPS
    gw = hg * SSM_HEAD_DIM
    n = SSM_STATE

    @pl.when(pl.program_id(1) == 0)
    def _():
        state[...] = jnp.zeros(state.shape, F32)

    dt = dt_ref[0]
    a = dt * (-jnp.exp(alog_ref[...]))
    row = lax.broadcasted_iota(I32, (L, L), 0)
    col = lax.broadcasted_iota(I32, (L, L), 1)
    causal = row >= col
    tri = causal.astype(F32)
    cum = jnp.dot(tri, a, precision=HIGHEST, preferred_element_type=F32)
    cum_t = cum.T
    dt_t = dt.T
    total = cum[L - 1:L, :]

    head_of = lax.broadcasted_iota(I32, (hp, d_inner), 1) // SSM_HEAD_DIM
    rep = (head_of == lax.broadcasted_iota(I32, (hp, d_inner), 0)).astype(BF16)
    grow = _expand_heads(jnp.exp(cum), rep)
    to_end = _expand_heads(jnp.exp(total - cum) * dt, rep)
    keep = _expand_heads(jnp.exp(total), rep)
    head_in_group = lax.broadcasted_iota(I32, (L, gw), 1) // SSM_HEAD_DIM

    xs = xs_ref[0]
    xf = xs.astype(F32)
    for g in range(SSM_GROUPS):
        bg = b_ref[0, :, g * n:(g + 1) * n]
        cg = c_ref[0, :, g * n:(g + 1) * n]
        xg = xs[:, g * gw:(g + 1) * gw]
        cb = lax.dot_general(cg, bg, (((1,), (1,)), ((), ())), preferred_element_type=F32)
        st = state[g]
        y_off = jnp.dot(cg, st.astype(BF16), preferred_element_type=F32)
        y_diag = jnp.zeros((L, gw), F32)
        for hh in range(hg):
            h = g * hg + hh
            seg = cum[:, h:h + 1] - cum_t[h:h + 1, :]
            w = jnp.where(causal, jnp.exp(seg), 0.0) * cb * dt_t[h:h + 1, :]
            x_head = jnp.where(head_in_group == hh, xg, jnp.zeros_like(xg))
            y_diag = y_diag + jnp.dot(w.astype(BF16), x_head, preferred_element_type=F32)
        sl = slice(g * gw, (g + 1) * gw)
        y = y_diag + y_off * grow[:, sl] + dskip_ref[:, sl] * xf[:, sl]
        xw = (xf[:, sl] * to_end[:, sl]).astype(BF16)
        upd = lax.dot_general(bg, xw, (((0,), (0,)), ((), ())), preferred_element_type=F32)
        state[g] = st * keep[:, sl] + upd
        y = y * _silu(z_ref[0, :, sl].astype(F32))
        y = y * lax.rsqrt(jnp.mean(y * y, axis=-1, keepdims=True) + EPS)
        o_ref[0, :, sl] = (y * ng_ref[:, sl]).astype(BF16)


def _ssd(xs, bm, cm, dt, z, a_log, d_skip, norm_g):
    bsz, s, d_inner = xs.shape
    nh = a_log.shape[0]
    hp = dt.shape[2]
    gn = bm.shape[2]
    L = SSD_CHUNK
    gw = d_inner // SSM_GROUPS
    consts = [jnp.pad(a_log.reshape(1, nh), ((0, 0), (0, hp - nh))),
              jnp.repeat(d_skip, SSM_HEAD_DIM).reshape(1, d_inner), norm_g.reshape(1, d_inner)]
    tile = lambda n: pl.BlockSpec((1, L, n), lambda b, j: (b, j, 0))
    full = lambda a: pl.BlockSpec(a.shape, lambda b, j: (0,) * a.ndim)
    return pl.pallas_call(
        _ssd_kernel,
        grid=(bsz, s // L),
        in_specs=[tile(d_inner), tile(gn), tile(gn), tile(hp), tile(d_inner)] + [full(a) for a in consts],
        out_specs=tile(d_inner),
        out_shape=jax.ShapeDtypeStruct((bsz, s, d_inner), BF16),
        scratch_shapes=[pltpu.VMEM((SSM_GROUPS, SSM_STATE, gw), F32)],
        compiler_params=_params(2),
        name="ssd",
    )(xs, bm, cm, dt, z, *consts)


def _out_proj_kernel(y_ref, x_ref, gate_ref, w_ref, o_ref):
    o_ref[0] = x_ref[0] + gate_ref[0] * jnp.dot(y_ref[0], w_ref[...], preferred_element_type=F32)


def _out_proj(y, x, gate, out_w):
    bsz, s, d = x.shape
    k = y.shape[2]
    ts = 512
    return pl.pallas_call(
        _out_proj_kernel,
        grid=(bsz, s // ts),
        in_specs=[pl.BlockSpec((1, ts, k), lambda b, j: (b, j, 0)), pl.BlockSpec((1, ts, d), lambda b, j: (b, j, 0)),
                  pl.BlockSpec((1, 1, d), lambda b, j: (b, 0, 0)), pl.BlockSpec((k, d), lambda b, j: (0, 0))],
        out_specs=pl.BlockSpec((1, ts, d), lambda b, j: (b, j, 0)),
        out_shape=jax.ShapeDtypeStruct((bsz, s, d), F32),
        compiler_params=_params(2),
        name="out_proj",
    )(y, x, gate, out_w.astype(BF16))


def _route_kernel(x_ref, sh_ref, sc_ref, ng_ref, wt_ref, rb_ref, h_ref, idx_ref, gate_ref, rank_ref, cnt_ref, running):
    tm = x_ref.shape[0]
    ne = wt_ref.shape[0]

    @pl.when(pl.program_id(0) == 0)
    def _():
        running[...] = jnp.zeros(running.shape, F32)

    h = _rms_modulate(x_ref[...], ng_ref[...], sh_ref[0], sc_ref[0])
    h_ref[...] = h
    h_hi, h_lo = _split_bf16(h)
    w_hi, w_lo = _split_bf16(wt_ref[...])
    nt = (((1,), (1,)), ((), ()))
    logits = (lax.dot_general(w_hi, h_hi, nt, preferred_element_type=F32)
              + lax.dot_general(w_hi, h_lo, nt, preferred_element_type=F32)
              + lax.dot_general(w_lo, h_hi, nt, preferred_element_type=F32)) + rb_ref[...]

    e_iota = lax.broadcasted_iota(I32, (ne, tm), 0).astype(F32)
    vals = logits
    tops, sels, idxs = [], [], []
    for _ in range(TOP_K):
        m = jnp.max(vals, axis=0, keepdims=True)
        idx = jnp.min(jnp.where(vals == m, e_iota, float(ne)), axis=0, keepdims=True)
        sel = e_iota == idx
        vals = jnp.where(sel, -jnp.inf, vals)
        tops.append(m)
        idxs.append(idx)
        sels.append(sel)
    exps = [jnp.exp(t - tops[0]) for t in tops]
    inv = 1.0 / (exps[0] + exps[1] + exps[2] + exps[3])

    member = jnp.zeros((ne, tm), F32)
    for sel in sels:
        member = member + sel.astype(F32)
    before = (lax.broadcasted_iota(I32, (tm, tm), 0) < lax.broadcasted_iota(I32, (tm, tm), 1)).astype(BF16)
    prior = jnp.dot(member.astype(BF16), before, preferred_element_type=F32) + running[...]
    for k in range(TOP_K):
        idx_ref[k:k + 1, :] = idxs[k].astype(I32)
        gate_ref[k:k + 1, :] = exps[k] * inv
        rank_ref[k:k + 1, :] = jnp.sum(jnp.where(sels[k], prior, 0.0), axis=0, keepdims=True).astype(I32)
    running[...] = running[...] + jnp.sum(member, axis=1, keepdims=True)
    cnt_ref[...] = jnp.broadcast_to(running[...], cnt_ref.shape).astype(I32)


def _route(x2, shift, scale, norm_g, router_w, router_b, tokens_per_batch):
    t, d = x2.shape
    ne = router_w.shape[1]
    tm = TOKEN_TILE
    tiles_per_batch = tokens_per_batch // tm
    per_batch = pl.BlockSpec((1, 1, d), lambda i: (i // tiles_per_batch, 0, 0))
    full = lambda shape: pl.BlockSpec(shape, lambda i: (0,) * len(shape))
    kt = pl.BlockSpec((TOP_K, tm), lambda i: (0, i))
    return pl.pallas_call(
        _route_kernel,
        grid=(t // tm,),
        in_specs=[pl.BlockSpec((tm, d), lambda i: (i, 0)), per_batch, per_batch,
                  full((1, d)), full((ne, d)), full((ne, 1))],
        out_specs=[pl.BlockSpec((tm, d), lambda i: (i, 0)), kt, kt, kt, full((ne, LANES))],
        out_shape=[jax.ShapeDtypeStruct((t, d), F32), jax.ShapeDtypeStruct((TOP_K, t), I32),
                   jax.ShapeDtypeStruct((TOP_K, t), F32), jax.ShapeDtypeStruct((TOP_K, t), I32),
                   jax.ShapeDtypeStruct((ne, LANES), I32)],
        scratch_shapes=[pltpu.VMEM((ne, 1), F32)],
        compiler_params=_params(1),
        name="route",
    )(x2, shift, scale, norm_g.reshape(1, d), router_w.T, router_b.reshape(ne, 1))


def _dispatch_kernel(pstart_ref, h_ref, idx_ref, rank_ref, init_ref, xb_ref, sem):
    del init_ref
    tm = h_ref.shape[0]

    def copy(t, k):
        dest = pstart_ref[idx_ref[k, t]] + rank_ref[k, t]
        return pltpu.make_async_copy(h_ref.at[pl.ds(t, 1)], xb_ref.at[pl.ds(dest, 1)], sem)

    def start(t, carry):
        for k in range(TOP_K):
            copy(t, k).start()
        return carry

    def wait(t, carry):
        for k in range(TOP_K):
            copy(t, k).wait()
        return carry

    lax.fori_loop(0, tm, start, 0)
    lax.fori_loop(0, tm, wait, 0)


def _dispatch(pstart, h, idx, rank, n_slots):
    t, d = h.shape
    tm = TOKEN_TILE
    smem_kt = pl.BlockSpec((TOP_K, tm), lambda i, p: (0, i), memory_space=pltpu.SMEM)
    return pl.pallas_call(
        _dispatch_kernel,
        grid_spec=pltpu.PrefetchScalarGridSpec(
            num_scalar_prefetch=1,
            grid=(t // tm,),
            in_specs=[pl.BlockSpec((tm, d), lambda i, p: (i, 0)), smem_kt, smem_kt,
                      pl.BlockSpec(memory_space=pl.ANY)],
            out_specs=pl.BlockSpec(memory_space=pl.ANY),
            scratch_shapes=[pltpu.SemaphoreType.DMA(())],
        ),
        out_shape=jax.ShapeDtypeStruct((n_slots, d), F32),
        input_output_aliases={4: 0},
        compiler_params=_params(1),
        name="dispatch",
    )(pstart, h, idx, rank, jnp.zeros((n_slots, d), F32))


def _experts_kernel(be_ref, nu_ref, x_ref, w1_ref, b1_ref, w2_ref, b2_ref, y_ref, w1b, w2b):
    i = pl.program_id(0)
    f = w2_ref.shape[1]
    used = i < nu_ref[0]
    changed = (i == 0) | (be_ref[i] != be_ref[jnp.maximum(i - 1, 0)])

    @pl.when(used & changed)
    def _():
        w1b[...] = w1_ref[0].astype(BF16)
        w2b[...] = w2_ref[0].astype(BF16)

    @pl.when(jnp.logical_not(used))
    def _():
        y_ref[...] = jnp.zeros(y_ref.shape, F32)

    @pl.when(used)
    def _():
        gu =jnp.dot(x_ref[...].astype(BF16), w1b[...], preferred_element_type=F32) + b1_ref[0]
        g = jnp.minimum(gu[:, :f], SWIGLU_LIMIT)
        u = jnp.clip(gu[:, f:], -SWIGLU_LIMIT, SWIGLU_LIMIT)
        act = (u + 1.0) * (g * _sigmoid(SWIGLU_ALPHA * g))
        y_ref[...] = jnp.dot(act.astype(BF16), w2b[...], preferred_element_type=F32) + b2_ref[0]


def _experts(block_e, n_used, xb, w1, b1, w2, b2):
    n_slots, d = xb.shape
    ne, _, f2 = w1.shape
    f = w2.shape[1]
    nb = n_slots // SLOT_BLOCK
    blk = lambda i, be, nu: jnp.minimum(i, nu[0] - 1)
    rows = pl.BlockSpec((SLOT_BLOCK, d), lambda i, be, nu: (blk(i, be, nu), 0))
    per_e = lambda shape: pl.BlockSpec((1,) + shape, lambda i, be, nu: (be[blk(i, be, nu)], 0, 0))
    return pl.pallas_call(
        _experts_kernel,
        grid_spec=pltpu.PrefetchScalarGridSpec(
            num_scalar_prefetch=2,
            grid=(nb,),
            in_specs=[rows, per_e((d, f2)), per_e((1, f2)), per_e((f, d)), per_e((1, d))],
            out_specs=pl.BlockSpec((SLOT_BLOCK, d), lambda i, be, nu: (i, 0)),
            scratch_shapes=[pltpu.VMEM((d, f2), BF16), pltpu.VMEM((f, d), BF16)],
        ),
        out_shape=jax.ShapeDtypeStruct((n_slots, d), F32),
        compiler_params=_params(1),
        name="experts",
    )(block_e, n_used, xb, w1, b1.reshape(ne, 1, f2), w2, b2.reshape(ne, 1, d))


def _combine_kernel(final_norm, pstart_ref, x_ref, idx_ref, rank_ref, gate_ref, g2_ref, fg_ref, ys_ref, o_ref,
                    buf, sem):
    tm = x_ref.shape[0]

    def copy(t, k):
        src = pstart_ref[idx_ref[k, t]] + rank_ref[k, t]
        return pltpu.make_async_copy(ys_ref.at[pl.ds(src, 1)], buf.at[k, pl.ds(t, 1)], sem)

    def start(t, carry):
        for k in range(TOP_K):
            copy(t, k).start()
        return carry

    def wait(t, carry):
        for k in range(TOP_K):
            copy(t, k).wait()
        return carry

    lax.fori_loop(0, tm, start, 0)
    lax.fori_loop(0, tm, wait, 0)
    gate = gate_ref[...]
    acc = gate[:, 0:1] * buf[0]
    for k in range(1, TOP_K):
        acc = acc + gate[:, k:k + 1] * buf[k]
    x = x_ref[...] + g2_ref[0] * acc
    if final_norm:
        x = x * lax.rsqrt(jnp.mean(x * x, axis=-1, keepdims=True) + EPS) * fg_ref[...]
    o_ref[...] = x


def _combine(pstart, x2, idx, rank, gate_t, g2, final_g, ys, tokens_per_batch, final_norm):
    t, d = x2.shape
    tm = TOKEN_TILE
    tiles_per_batch = tokens_per_batch // tm
    smem_kt = pl.BlockSpec((TOP_K, tm), lambda i, p: (0, i), memory_space=pltpu.SMEM)
    return pl.pallas_call(
        functools.partial(_combine_kernel, final_norm),
        grid_spec=pltpu.PrefetchScalarGridSpec(
            num_scalar_prefetch=1,
            grid=(t // tm,),
            in_specs=[pl.BlockSpec((tm, d), lambda i, p: (i, 0)), smem_kt, smem_kt,
                      pl.BlockSpec((tm, TOP_K), lambda i, p: (i, 0)),
                      pl.BlockSpec((1, 1, d), lambda i, p: (i // tiles_per_batch, 0, 0)),
                      pl.BlockSpec((1, d), lambda i, p: (0, 0)),
                      pl.BlockSpec(memory_space=pl.ANY)],
            out_specs=pl.BlockSpec((tm, d), lambda i, p: (i, 0)),
            scratch_shapes=[pltpu.VMEM((TOP_K, tm, d), F32), pltpu.SemaphoreType.DMA(())],
        ),
        out_shape=jax.ShapeDtypeStruct((t, d), F32),
        compiler_params=_params(1),
        name="combine",
    )(pstart, x2, idx, rank, gate_t, g2, final_g.reshape(1, d), ys)


def _moe(x, shift, scale, gate2, norm_g, router_w, router_b, w1, b1, w2, b2, final_g, final_norm):
    bsz, s, d = x.shape
    t = bsz * s
    ne = router_w.shape[1]
    x2 = x.reshape(t, d)
    h, idx, gate, rank, counts = _route(x2, shift, scale, norm_g, router_w, router_b, s)
    counts = counts[:, 0]
    padded = ((counts + SLOT_BLOCK - 1) // SLOT_BLOCK) * SLOT_BLOCK
    ends = jnp.cumsum(padded)
    pstart = (ends - padded).astype(I32)
    n_slots = t * TOP_K + ne * SLOT_BLOCK
    nb = n_slots // SLOT_BLOCK
    block_e = jnp.minimum(jnp.searchsorted(ends, jnp.arange(nb, dtype=I32) * SLOT_BLOCK, side="right"),
                          ne - 1).astype(I32)
    n_used = (ends[-1:] // SLOT_BLOCK).astype(I32)
    xb = _dispatch(pstart, h, idx, rank, n_slots)
    ys = _experts(block_e, n_used, xb, w1, b1, w2, b2)
    out = _combine(pstart, x2, idx, rank, gate.T, gate2, final_g, ys, s, final_norm)
    return out.reshape(bsz, s, d)


def kernel(x, c, ada_w, ada_b, mix_norm_g, ffn_norm_g, conf_pw1_w, conf_pw1_b, conf_dw_w, conf_dw_b, conf_ln_g, conf_ln_b, conf_pw2_w, conf_pw2_b, ssm_in_w, ssm_conv_w, ssm_conv_b, ssm_dt_bias, ssm_a_log, ssm_d, ssm_norm_g, ssm_out_w, router_w, router_b, exp_w1, exp_b1, exp_w2, exp_b2, final_norm_g):
    depth = ada_w.shape[0]
    bsz, s, d = x.shape
    mod = _ada(c, ada_w, ada_b).reshape(depth, bsz, 6, 1, d)
    for i in range(depth):
        sh1, sc1, g1, sh2, sc2, g2 = (mod[i, :, m] for m in range(6))
        j = i // 2
        if i % 2 == 0:
            x = _conformer(x, sh1, sc1, g1, mix_norm_g[i], conf_pw1_w[j], conf_pw1_b[j], conf_dw_w[j], conf_dw_b[j],
                           conf_ln_g[j], conf_ln_b[j], conf_pw2_w[j], conf_pw2_b[j])
        else:
            z, xs, bm, cm, dt = _ssm_in(x, sh1, sc1, mix_norm_g[i], ssm_in_w[j], ssm_conv_w[j], ssm_conv_b[j],
                                        ssm_dt_bias[j])
            y = _ssd(xs, bm, cm, dt, z, ssm_a_log[j], ssm_d[j], ssm_norm_g[j])
            x = _out_proj(y, x, g1, ssm_out_w[j])
        x = _moe(x, sh2, sc2, g2, ffn_norm_g[i], router_w[i], router_b[i], exp_w1[i], exp_b1[i], exp_w2[i],
                 exp_b2[i], final_norm_g, final_norm=(i == depth - 1))
    return x
```

```python
import functools

import jax
import jax.numpy as jnp
from jax import lax
from jax.experimental import pallas as pl
from jax.experimental.pallas import tpu as pltpu

F32 = jnp.float32
BF16 = jnp.bfloat16
I32 = jnp.int32

EPS = 1e-5
CONV_WIDTH = 31
SSM_HEAD_DIM = 64
SSM_GROUPS = 8
SSM_STATE = 128
SSM_CONV = 4
SSD_CHUNK = 128
N_EXPERTS = 32
TOP_K = 4
SWIGLU_LIMIT = 7.0
SWIGLU_ALPHA = 1.702

LANES = 128
SUBLANES = 8
TILE_CH = LANES * SUBLANES
SLOT_BLOCK = 256
TOKEN_TILE = 256
SEQ_TILE = 256
HALO = 32
SSM_HALO = 8
CONV_GROUP = 8
VMEM_LIMIT = 56 * 1024 * 1024

HIGHEST = lax.Precision.HIGHEST


def _params(n_axes, vmem=VMEM_LIMIT):
    return pltpu.CompilerParams(dimension_semantics=("arbitrary",) * n_axes, vmem_limit_bytes=vmem)


def _sigmoid(v):
    return 1.0 / (1.0 + jnp.exp(-v))


def _silu(v):
    return v * _sigmoid(v)


def _rms_modulate(x, g, shift, scale):
    ms = jnp.mean(x * x, axis=-1, keepdims=True)
    return (x * lax.rsqrt(ms + EPS) * g) * (1.0 + scale) + shift


def _split_bf16(v):
    hi = v.astype(BF16)
    lo = (v - hi.astype(F32)).astype(BF16)
    return hi, lo


def _store_token_major(ref, tok0, v):
    n, d = v.shape
    assert d == TILE_CH and n % SUBLANES == 0
    for th in range(n // SUBLANES):
        for j in range(SUBLANES):
            ref[pl.ds((tok0 + th * SUBLANES) * SUBLANES + j, SUBLANES, stride=SUBLANES), :] = (
                v[th * SUBLANES:(th + 1) * SUBLANES, j * LANES:(j + 1) * LANES])


def _load_token_major(ref, tok0, n):
    rows = []
    for th in range(n // SUBLANES):
        rows.append(jnp.concatenate(
            [ref[pl.ds((tok0 + th * SUBLANES) * SUBLANES + j, SUBLANES, stride=SUBLANES), :] for j in range(SUBLANES)],
            axis=1))
    return jnp.concatenate(rows, axis=0)


def _tile_rows(tok):
    return pl.ds(pl.multiple_of(tok * SUBLANES, SUBLANES), SUBLANES)


def _token_major_conv(src, dst, w_ref, b_tile, n_tokens, width, halo, post):
    base = halo - (width - 1)

    def group(gi, carry):
        t0 = gi * CONV_GROUP
        tiles = [src[_tile_rows(t0 + base + i), :] for i in range(width + CONV_GROUP - 1)]
        for o in range(CONV_GROUP):
            acc = [b_tile, None]
            for k in range(width):
                term = w_ref[k * SUBLANES:(k + 1) * SUBLANES, :] * tiles[o + k]
                acc[k % 2] = term if acc[k % 2] is None else acc[k % 2] + term
            dst[_tile_rows(t0 + o), :] = post(acc[0] + acc[1])
        return carry

    lax.fori_loop(0, n_tokens // CONV_GROUP, group, 0)


def _ada_kernel(c_ref, w_ref, b_ref, o_ref):
    c = c_ref[...]
    o_ref[0] = jnp.dot(_silu(c), w_ref[0], precision=HIGHEST, preferred_element_type=F32) + b_ref[0]


def _ada(c, ada_w, ada_b):
    depth, d, n = ada_w.shape
    bsz = c.shape[0]
    tn = 1536
    return pl.pallas_call(
        _ada_kernel,
        grid=(depth, n // tn),
        in_specs=[pl.BlockSpec((bsz, d), lambda l, j: (0, 0)),
                  pl.BlockSpec((1, d, tn), lambda l, j: (l, 0, j)),
                  pl.BlockSpec((1, 1, tn), lambda l, j: (l, 0, j))],
        out_specs=pl.BlockSpec((1, bsz, tn), lambda l, j: (l, 0, j)),
        out_shape=jax.ShapeDtypeStruct((depth, bsz, n), F32),
        compiler_params=_params(2),
        name="ada",
    )(c, ada_w, ada_b.reshape(depth, 1, n))


def _conformer_kernel(x_ref, sh_ref, sc_ref, gate_ref, ng_ref, w1_ref, b1_ref, dw_ref, dwb_ref,
                      lng_ref, lnb_ref, w2_ref, b2_ref, o_ref, ubuf, cbuf):
    ts, d = x_ref.shape[1], x_ref.shape[2]
    x = x_ref[0]
    h = _rms_modulate(x, ng_ref[...], sh_ref[0], sc_ref[0])
    u = jnp.dot(h.astype(BF16), w1_ref[...], preferred_element_type=F32) + b1_ref[...]
    u = u[:, :d] * _sigmoid(u[:, d:])

    @pl.when(pl.program_id(1) == 0)
    def _():
        ubuf[0:HALO * SUBLANES, :] = jnp.zeros((HALO * SUBLANES, LANES), F32)

    _store_token_major(ubuf, HALO, u)
    _token_major_conv(ubuf, cbuf, dw_ref, dwb_ref[...], ts, CONV_WIDTH, HALO, lambda a: a)
    ubuf[0:HALO * SUBLANES, :] = ubuf[ts * SUBLANES:(ts + HALO) * SUBLANES, :]

    v = _load_token_major(cbuf, 0, ts)
    mu = jnp.mean(v, axis=-1, keepdims=True)
    vc = v - mu
    var = jnp.mean(vc * vc, axis=-1, keepdims=True)
    v = _silu(vc * lax.rsqrt(var + EPS) * lng_ref[...] + lnb_ref[...])
    y = jnp.dot(v.astype(BF16), w2_ref[...], preferred_element_type=F32) + b2_ref[...]
    o_ref[0] = x + gate_ref[0] * y


def _conformer(x, shift, scale, gate, norm_g, pw1_w, pw1_b, dw_w, dw_b, ln_g, ln_b, pw2_w, pw2_b):
    bsz, s, d = x.shape
    assert d == TILE_CH
    ts = SEQ_TILE
    row = lambda a: a.reshape(1, -1)
    per_batch = pl.BlockSpec((1, 1, d), lambda b, j: (b, 0, 0))
    full = lambda a: pl.BlockSpec(a.shape, lambda b, j: (0,) * a.ndim)
    consts = [row(norm_g), pw1_w.astype(BF16), row(pw1_b), dw_w.reshape(CONV_WIDTH * SUBLANES, LANES),
              dw_b.reshape(SUBLANES, LANES), row(ln_g), row(ln_b), pw2_w.astype(BF16), row(pw2_b)]
    return pl.pallas_call(
        _conformer_kernel,
        grid=(bsz, s // ts),
        in_specs=[pl.BlockSpec((1, ts, d), lambda b, j: (b, j, 0)), per_batch, per_batch, per_batch]
                 + [full(a) for a in consts],
        out_specs=pl.BlockSpec((1, ts, d), lambda b, j: (b, j, 0)),
        out_shape=jax.ShapeDtypeStruct((bsz, s, d), F32),
        scratch_shapes=[pltpu.VMEM(((HALO + ts) * SUBLANES, LANES), F32), pltpu.VMEM((ts * SUBLANES, LANES), F32)],
        compiler_params=_params(2),
        name="conformer",
    )(x, shift, scale, gate, *consts)


def _ssm_in_kernel(x_ref, sh_ref, sc_ref, ng_ref, wz_ref, wx_ref, wdt_ref, cw_ref, cb_ref, dtb_ref,
                   z_ref, xs_ref, b_ref, c_ref, dt_ref, buf, cbuf):
    ts = x_ref.shape[1]
    d_inner = xs_ref.shape[2]
    gn = b_ref.shape[2]
    n_col = buf.shape[0]
    nh = d_inner // SSM_HEAD_DIM
    h = _rms_modulate(x_ref[0], ng_ref[...], sh_ref[0], sc_ref[0]).astype(BF16)
    z_ref[0] = jnp.dot(h, wz_ref[...], preferred_element_type=F32).astype(BF16)
    dt = jnp.dot(h, wdt_ref[...], preferred_element_type=F32) + dtb_ref[...]
    dt = jnp.maximum(dt, 0.0) + jnp.log(1.0 + jnp.exp(-jnp.abs(dt)))
    dt_ref[0] = jnp.where(lax.broadcasted_iota(I32, dt.shape, 1) < nh, dt, 0.0)

    @pl.when(pl.program_id(1) == 0)
    def _():
        buf[:, 0:SSM_HALO * SUBLANES, :] = jnp.zeros((n_col, SSM_HALO * SUBLANES, LANES), F32)

    outs = []
    for q in range(d_inner // TILE_CH):
        outs.append((xs_ref, q * TILE_CH))
    for q in range(gn // TILE_CH):
        outs.append((b_ref, q * TILE_CH))
    for q in range(gn // TILE_CH):
        outs.append((c_ref, q * TILE_CH))
    for q in range(n_col):
        xq = jnp.dot(h, wx_ref[:, q * TILE_CH:(q + 1) * TILE_CH], preferred_element_type=F32)
        _store_token_major(buf.at[q], SSM_HALO, xq)
        _token_major_conv(buf.at[q], cbuf, cw_ref.at[q], cb_ref[q], ts, SSM_CONV, SSM_HALO, _silu)
        buf[q, 0:SSM_HALO * SUBLANES, :] = buf[q, ts * SUBLANES:(ts + SSM_HALO) * SUBLANES, :]
        ref, lo = outs[q]
        ref[0, :, lo:lo + TILE_CH] = _load_token_major(cbuf, 0, ts).astype(BF16)


def _ssm_in(x, shift, scale, norm_g, in_w, conv_w, conv_b, dt_bias):
    bsz, s, d = x.shape
    ts = SEQ_TILE
    nh = dt_bias.shape[0]
    d_inner = nh * SSM_HEAD_DIM
    gn = SSM_GROUPS * SSM_STATE
    conv_dim = d_inner + 2 * gn
    n_col = conv_dim // TILE_CH
    assert d_inner % TILE_CH == 0 and gn % TILE_CH == 0
    w = in_w.astype(BF16)
    lane_pad = lambda a: jnp.pad(a, ((0, 0), (0, LANES - nh)))
    cw = conv_w.reshape(SSM_CONV, n_col, SUBLANES, LANES).transpose(1, 0, 2, 3).reshape(n_col, SSM_CONV * SUBLANES, LANES)
    cb = conv_b.reshape(n_col, SUBLANES, LANES)
    consts = [norm_g.reshape(1, d), w[:, :d_inner], w[:, d_inner:d_inner + conv_dim],
              lane_pad(w[:, d_inner + conv_dim:]), cw, cb, lane_pad(dt_bias.reshape(1, nh))]
    per_batch = pl.BlockSpec((1, 1, d), lambda b, j: (b, 0, 0))
    full = lambda a: pl.BlockSpec(a.shape, lambda b, j: (0,) * a.ndim)
    tile = lambda n: pl.BlockSpec((1, ts, n), lambda b, j: (b, j, 0))
    return pl.pallas_call(
        _ssm_in_kernel,
        grid=(bsz, s // ts),
        in_specs=[tile(d), per_batch, per_batch] + [full(a) for a in consts],
        out_specs=[tile(d_inner), tile(d_inner), tile(gn), tile(gn), tile(LANES)],
        out_shape=[jax.ShapeDtypeStruct((bsz, s, d_inner), BF16), jax.ShapeDtypeStruct((bsz, s, d_inner), BF16),
                   jax.ShapeDtypeStruct((bsz, s, gn), BF16), jax.ShapeDtypeStruct((bsz, s, gn), BF16),
                   jax.ShapeDtypeStruct((bsz, s, LANES), F32)],
        scratch_shapes=[pltpu.VMEM((n_col, (SSM_HALO + ts) * SUBLANES, LANES), F32),
                        pltpu.VMEM((ts * SUBLANES, LANES), F32)],
        compiler_params=_params(2),
        name="ssm_in",
    )(x, shift, scale, *consts)


def _expand_heads(v, rep):
    hi, lo = _split_bf16(v)
    return (jnp.dot(hi, rep, preferred_element_type=F32) + jnp.dot(lo, rep, preferred_element_type=F32))


def _ssd_kernel(xs_ref, b_ref, c_ref, dt_ref, z_ref, alog_ref, dskip_ref, ng_ref, o_ref, state):
    L = xs_ref.shape[1]
    d_inner = xs_ref.shape[2]
    hp = dt_ref.shape[2]
    hg = d_inner // SSM_HEAD_DIM // SSM_GROUPS
    gw = hg * SSM_HEAD_DIM
    n = SSM_STATE

    @pl.when(pl.program_id(1) == 0)
    def _():
        state[...] = jnp.zeros(state.shape, F32)

    dt = dt_ref[0]
    a = dt * (-jnp.exp(alog_ref[...]))
    row = lax.broadcasted_iota(I32, (L, L), 0)
    col = lax.broadcasted_iota(I32, (L, L), 1)
    causal = row >= col
    tri = causal.astype(F32)
    cum = jnp.dot(tri, a, precision=HIGHEST, preferred_element_type=F32)
    cum_t = cum.T
    dt_t = dt.T
    total = cum[L - 1:L, :]

    head_of = lax.broadcasted_iota(I32, (hp, d_inner), 1) // SSM_HEAD_DIM
    rep = (head_of == lax.broadcasted_iota(I32, (hp, d_inner), 0)).astype(BF16)
    grow = _expand_heads(jnp.exp(cum), rep)
    to_end = _expand_heads(jnp.exp(total - cum) * dt, rep)
    keep = _expand_heads(jnp.exp(total), rep)
    head_in_group = lax.broadcasted_iota(I32, (L, gw), 1) // SSM_HEAD_DIM

    xs = xs_ref[0]
    xf = xs.astype(F32)
    for g in range(SSM_GROUPS):
        bg = b_ref[0, :, g * n:(g + 1) * n]
        cg = c_ref[0, :, g * n:(g + 1) * n]
        xg = xs[:, g * gw:(g + 1) * gw]
        cb = lax.dot_general(cg, bg, (((1,), (1,)), ((), ())), preferred_element_type=F32)
        st = state[g]
        y_off = jnp.dot(cg, st.astype(BF16), preferred_element_type=F32)
        y_diag = jnp.zeros((L, gw), F32)
        for hh in range(hg):
            h = g * hg + hh
            seg = cum[:, h:h + 1] - cum_t[h:h + 1, :]
            w = jnp.where(causal, jnp.exp(seg), 0.0) * cb * dt_t[h:h + 1, :]
            x_head = jnp.where(head_in_group == hh, xg, jnp.zeros_like(xg))
            y_diag = y_diag + jnp.dot(w.astype(BF16), x_head, preferred_element_type=F32)
        sl = slice(g * gw, (g + 1) * gw)
        y = y_diag + y_off * grow[:, sl] + dskip_ref[:, sl] * xf[:, sl]
        xw = (xf[:, sl] * to_end[:, sl]).astype(BF16)
        upd = lax.dot_general(bg, xw, (((0,), (0,)), ((), ())), preferred_element_type=F32)
        state[g] = st * keep[:, sl] + upd
        y = y * _silu(z_ref[0, :, sl].astype(F32))
        y = y * lax.rsqrt(jnp.mean(y * y, axis=-1, keepdims=True) + EPS)
        o_ref[0, :, sl] = (y * ng_ref[:, sl]).astype(BF16)


def _ssd(xs, bm, cm, dt, z, a_log, d_skip, norm_g):
    bsz, s, d_inner = xs.shape
    nh = a_log.shape[0]
    hp = dt.shape[2]
    gn = bm.shape[2]
    L = SSD_CHUNK
    gw = d_inner // SSM_GROUPS
    consts = [jnp.pad(a_log.reshape(1, nh), ((0, 0), (0, hp - nh))),
              jnp.repeat(d_skip, SSM_HEAD_DIM).reshape(1, d_inner), norm_g.reshape(1, d_inner)]
    tile = lambda n: pl.BlockSpec((1, L, n), lambda b, j: (b, j, 0))
    full = lambda a: pl.BlockSpec(a.shape, lambda b, j: (0,) * a.ndim)
    return pl.pallas_call(
        _ssd_kernel,
        grid=(bsz, s // L),
        in_specs=[tile(d_inner), tile(gn), tile(gn), tile(hp), tile(d_inner)] + [full(a) for a in consts],
        out_specs=tile(d_inner),
        out_shape=jax.ShapeDtypeStruct((bsz, s, d_inner), BF16),
        scratch_shapes=[pltpu.VMEM((SSM_GROUPS, SSM_STATE, gw), F32)],
        compiler_params=_params(2),
        name="ssd",
    )(xs, bm, cm, dt, z, *consts)


def _out_proj_kernel(y_ref, x_ref, gate_ref, w_ref, o_ref):
    o_ref[0] = x_ref[0] + gate_ref[0] * jnp.dot(y_ref[0], w_ref[...], preferred_element_type=F32)


def _out_proj(y, x, gate, out_w):
    bsz, s, d = x.shape
    k = y.shape[2]
    ts = 512
    return pl.pallas_call(
        _out_proj_kernel,
        grid=(bsz, s // ts),
        in_specs=[pl.BlockSpec((1, ts, k), lambda b, j: (b, j, 0)), pl.BlockSpec((1, ts, d), lambda b, j: (b, j, 0)),
                  pl.BlockSpec((1, 1, d), lambda b, j: (b, 0, 0)), pl.BlockSpec((k, d), lambda b, j: (0, 0))],
        out_specs=pl.BlockSpec((1, ts, d), lambda b, j: (b, j, 0)),
        out_shape=jax.ShapeDtypeStruct((bsz, s, d), F32),
        compiler_params=_params(2),
        name="out_proj",
    )(y, x, gate, out_w.astype(BF16))


def _route_kernel(x_ref, sh_ref, sc_ref, ng_ref, wt_ref, rb_ref, h_ref, idx_ref, gate_ref, rank_ref, cnt_ref, running):
    tm = x_ref.shape[0]
    ne = wt_ref.shape[0]

    @pl.when(pl.program_id(0) == 0)
    def _():
        running[...] = jnp.zeros(running.shape, F32)

    h = _rms_modulate(x_ref[...], ng_ref[...], sh_ref[0], sc_ref[0])
    _store_token_major(h_ref, 0, h)
    h_hi, h_lo = _split_bf16(h)
    w_hi, w_lo = _split_bf16(wt_ref[...])
    nt = (((1,), (1,)), ((), ()))
    logits = (lax.dot_general(w_hi, h_hi, nt, preferred_element_type=F32)
              + lax.dot_general(w_hi, h_lo, nt, preferred_element_type=F32)
              + lax.dot_general(w_lo, h_hi, nt, preferred_element_type=F32)) + rb_ref[...]

    e_iota = lax.broadcasted_iota(I32, (ne, tm), 0).astype(F32)
    vals = logits
    tops, sels, idxs = [], [], []
    for _ in range(TOP_K):
        m = jnp.max(vals, axis=0, keepdims=True)
        idx = jnp.min(jnp.where(vals == m, e_iota, float(ne)), axis=0, keepdims=True)
        sel = e_iota == idx
        vals = jnp.where(sel, -jnp.inf, vals)
        tops.append(m)
        idxs.append(idx)
        sels.append(sel)
    exps = [jnp.exp(t - tops[0]) for t in tops]
    inv = 1.0 / (exps[0] + exps[1] + exps[2] + exps[3])

    member = jnp.zeros((ne, tm), F32)
    for sel in sels:
        member = member + sel.astype(F32)
    before = (lax.broadcasted_iota(I32, (tm, tm), 0) < lax.broadcasted_iota(I32, (tm, tm), 1)).astype(BF16)
    prior = jnp.dot(member.astype(BF16), before, preferred_element_type=F32) + running[...]
    for k in range(TOP_K):
        idx_ref[k:k + 1, :] = idxs[k].astype(I32)
        gate_ref[k:k + 1, :] = exps[k] * inv
        rank_ref[k:k + 1, :] = jnp.sum(jnp.where(sels[k], prior, 0.0), axis=0, keepdims=True).astype(I32)
    running[...] = running[...] + jnp.sum(member, axis=1, keepdims=True)
    cnt_ref[...] = jnp.broadcast_to(running[...], cnt_ref.shape).astype(I32)


def _route(x2, shift, scale, norm_g, router_w, router_b, tokens_per_batch):
    t, d = x2.shape
    assert d == TILE_CH
    ne = router_w.shape[1]
    tm = TOKEN_TILE
    tiles_per_batch = tokens_per_batch // tm
    per_batch = pl.BlockSpec((1, 1, d), lambda i: (i // tiles_per_batch, 0, 0))
    full = lambda shape: pl.BlockSpec(shape, lambda i: (0,) * len(shape))
    kt = pl.BlockSpec((TOP_K, tm), lambda i: (0, i))
    return pl.pallas_call(
        _route_kernel,
        grid=(t // tm,),
        in_specs=[pl.BlockSpec((tm, d), lambda i: (i, 0)), per_batch, per_batch,
                  full((1, d)), full((ne, d)), full((ne, 1))],
        out_specs=[pl.BlockSpec((tm * SUBLANES, LANES), lambda i: (i, 0)), kt, kt, kt, full((ne, LANES))],
        out_shape=[jax.ShapeDtypeStruct((t * SUBLANES, LANES), F32), jax.ShapeDtypeStruct((TOP_K, t), I32),
                   jax.ShapeDtypeStruct((TOP_K, t), F32), jax.ShapeDtypeStruct((TOP_K, t), I32),
                   jax.ShapeDtypeStruct((ne, LANES), I32)],
        scratch_shapes=[pltpu.VMEM((ne, 1), F32)],
        compiler_params=_params(1),
        name="route",
    )(x2, shift, scale, norm_g.reshape(1, d), router_w.T, router_b.reshape(ne, 1))


def _dest_kernel(pstart_ref, idx_ref, rank_ref, o_ref):
    idx = idx_ref[...]
    dest = rank_ref[...]
    for e in range(pstart_ref.shape[0]):
        dest = dest + jnp.where(idx == e, pstart_ref[e], 0)
    o_ref[...] = dest


def _dest(pstart, idx, rank):
    k, t = idx.shape
    tn = min(t, 4096)
    blk = pl.BlockSpec((k, tn), lambda i, p: (0, i))
    return pl.pallas_call(
        _dest_kernel,
        grid_spec=pltpu.PrefetchScalarGridSpec(num_scalar_prefetch=1, grid=(t // tn,), in_specs=[blk, blk],
                                               out_specs=blk),
        out_shape=jax.ShapeDtypeStruct((k, t), I32),
        compiler_params=_params(1),
        name="dest",
    )(pstart, idx, rank)


def _dispatch_kernel(zslot_ref, h_ref, dest_ref, xb_ref, zeros, sem, zsem):
    tm = h_ref.shape[0] // SUBLANES
    block_rows = SLOT_BLOCK * SUBLANES

    @pl.when(pl.program_id(0) == 0)
    def _():
        zeros[...] = jnp.zeros(zeros.shape, F32)

        def zcopy(e):
            rows = pl.ds(pl.multiple_of(zslot_ref[e] * SUBLANES, block_rows), block_rows)
            return pltpu.make_async_copy(zeros, xb_ref.at[rows], zsem)

        def zstart(e, carry):
            @pl.when(zslot_ref[e] >= 0)
            def _():
                zcopy(e).start()
            return carry

        def zwait(e, carry):
            @pl.when(zslot_ref[e] >= 0)
            def _():
                zcopy(e).wait()
            return carry

        lax.fori_loop(0, zslot_ref.shape[0], zstart, 0)
        lax.fori_loop(0, zslot_ref.shape[0], zwait, 0)

    def copy(t, k):
        return pltpu.make_async_copy(h_ref.at[_tile_rows(t)], xb_ref.at[_tile_rows(dest_ref[t * TOP_K + k])], sem)

    def start(t, carry):
        for k in range(TOP_K):
            copy(t, k).start()
        return carry

    def wait(t, carry):
        for k in range(TOP_K):
            copy(t, k).wait()
        return carry

    lax.fori_loop(0, tm, start, 0, unroll=4)
    lax.fori_loop(0, tm, wait, 0)


def _dispatch(zslot, h, dest_flat, n_slots):
    t = h.shape[0] // SUBLANES
    tm = TOKEN_TILE
    return pl.pallas_call(
        _dispatch_kernel,
        grid_spec=pltpu.PrefetchScalarGridSpec(
            num_scalar_prefetch=1,
            grid=(t // tm,),
            in_specs=[pl.BlockSpec((tm * SUBLANES, LANES), lambda i, z: (i, 0)),
                      pl.BlockSpec((tm * TOP_K,), lambda i, z: (i,), memory_space=pltpu.SMEM)],
            out_specs=pl.BlockSpec(memory_space=pl.ANY),
            scratch_shapes=[pltpu.VMEM((SLOT_BLOCK * SUBLANES, LANES), F32), pltpu.SemaphoreType.DMA(()),
                            pltpu.SemaphoreType.DMA(())],
        ),
        out_shape=jax.ShapeDtypeStruct((n_slots * SUBLANES, LANES), F32),
        compiler_params=_params(1),
        name="dispatch",
    )(zslot, h, dest_flat)


def _experts_kernel(be_ref, nu_ref, x_ref, w1_ref, b1_ref, w2_ref, b2_ref, y_ref, w1b, w2b):
    i = pl.program_id(0)
    f = w2b.shape[0]
    used = i < nu_ref[0]
    changed = (i == 0) | (be_ref[i] != be_ref[jnp.maximum(i - 1, 0)])

    @pl.when(used & changed)
    def _():
        w1b[...] = w1_ref[0, 0].astype(BF16)
        w2b[...] = w2_ref[0, 0].astype(BF16)

    @pl.when(jnp.logical_not(used))
    def _():
        y_ref[...] = jnp.zeros(y_ref.shape, F32)

    @pl.when(used)
    def _():
        x = _load_token_major(x_ref, 0, SLOT_BLOCK).astype(BF16)
        gu = jnp.dot(x, w1b[...], preferred_element_type=F32) + b1_ref[0, 0]
        g = jnp.minimum(gu[:, :f], SWIGLU_LIMIT)
        u = jnp.clip(gu[:, f:], -SWIGLU_LIMIT, SWIGLU_LIMIT)
        act = (u + 1.0) * (g * _sigmoid(SWIGLU_ALPHA * g))
        y = jnp.dot(act.astype(BF16), w2b[...], preferred_element_type=F32) + b2_ref[0, 0]
        _store_token_major(y_ref, 0, y)


def _experts(layer, block_e, n_used, xb, w1, b1, w2, b2):
    n_slots = xb.shape[0] // SUBLANES
    _, ne, d, f2 = w1.shape
    f = w2.shape[2]
    assert d == TILE_CH
    nb = n_slots // SLOT_BLOCK
    blk = lambda i, be, nu: jnp.minimum(i, nu[0] - 1)
    rows = (SLOT_BLOCK * SUBLANES, LANES)
    per_e = lambda shape: pl.BlockSpec((1, 1) + shape, lambda i, be, nu: (layer, be[blk(i, be, nu)], 0, 0))
    return pl.pallas_call(
        _experts_kernel,
        grid_spec=pltpu.PrefetchScalarGridSpec(
            num_scalar_prefetch=2,
            grid=(nb,),
            in_specs=[pl.BlockSpec(rows, lambda i, be, nu: (blk(i, be, nu), 0)),
                      per_e((d, f2)), per_e((1, f2)), per_e((f, d)), per_e((1, d))],
            out_specs=pl.BlockSpec(rows, lambda i, be, nu: (i, 0)),
            scratch_shapes=[pltpu.VMEM((d, f2), BF16), pltpu.VMEM((f, d), BF16)],
        ),
        out_shape=jax.ShapeDtypeStruct((n_slots * SUBLANES, LANES), F32),
        compiler_params=_params(1),
        name="experts",
    )(block_e, n_used, xb, w1, b1.reshape(b1.shape[0], ne, 1, f2), w2, b2.reshape(b2.shape[0], ne, 1, d))


def _combine_kernel(final_norm, x_ref, dest_ref, gate_ref, g2_ref, fg_ref, ys_ref, o_ref, buf, sem):
    tm = x_ref.shape[0]

    def copy(t, k):
        return pltpu.make_async_copy(ys_ref.at[_tile_rows(dest_ref[t * TOP_K + k])], buf.at[k, _tile_rows(t)], sem)

    def start(t, carry):
        for k in range(TOP_K):
            copy(t, k).start()
        return carry

    def wait(t, carry):
        for k in range(TOP_K):
            copy(t, k).wait()
        return carry

    lax.fori_loop(0, tm, start, 0, unroll=4)
    lax.fori_loop(0, tm, wait, 0)
    gate = gate_ref[...]
    acc = gate[:, 0:1] * _load_token_major(buf.at[0], 0, tm)
    for k in range(1, TOP_K):
        acc = acc + gate[:, k:k + 1] * _load_token_major(buf.at[k], 0, tm)
    x = x_ref[...] + g2_ref[0] * acc
    if final_norm:
        x = x * lax.rsqrt(jnp.mean(x * x, axis=-1, keepdims=True) + EPS) * fg_ref[...]
    o_ref[...] = x


def _combine(x2, dest_flat, gate_t, g2, final_g, ys, tokens_per_batch, final_norm):
    t, d = x2.shape
    tm = TOKEN_TILE
    tiles_per_batch = tokens_per_batch // tm
    return pl.pallas_call(
        functools.partial(_combine_kernel, final_norm),
        grid=(t // tm,),
        in_specs=[pl.BlockSpec((tm, d), lambda i: (i, 0)),
                  pl.BlockSpec((tm * TOP_K,), lambda i: (i,), memory_space=pltpu.SMEM),
                  pl.BlockSpec((tm, TOP_K), lambda i: (i, 0)),
                  pl.BlockSpec((1, 1, d), lambda i: (i // tiles_per_batch, 0, 0)),
                  pl.BlockSpec((1, d), lambda i: (0, 0)),
                  pl.BlockSpec(memory_space=pl.ANY)],
        out_specs=pl.BlockSpec((tm, d), lambda i: (i, 0)),
        out_shape=jax.ShapeDtypeStruct((t, d), F32),
        scratch_shapes=[pltpu.VMEM((TOP_K, tm * SUBLANES, LANES), F32), pltpu.SemaphoreType.DMA(())],
        compiler_params=_params(1),
        name="combine",
    )(x2, dest_flat, gate_t, g2, final_g.reshape(1, d), ys)


def _moe(layer, x, shift, scale, gate2, norm_g, router_w, router_b, w1, b1, w2, b2, final_g, final_norm):
    bsz, s, d = x.shape
    t = bsz * s
    ne = router_w.shape[1]
    x2 = x.reshape(t, d)
    h, idx, gate, rank, counts = _route(x2, shift, scale, norm_g, router_w, router_b, s)
    counts = counts[:, 0]
    padded = ((counts + SLOT_BLOCK - 1) // SLOT_BLOCK) * SLOT_BLOCK
    ends = jnp.cumsum(padded)
    pstart = (ends - padded).astype(I32)
    n_slots = t * TOP_K + ne * SLOT_BLOCK
    nb = n_slots // SLOT_BLOCK
    block_start = jnp.arange(nb, dtype=I32) * SLOT_BLOCK
    block_e = jnp.minimum(jnp.sum(block_start[:, None] >= ends[None, :], axis=1), ne - 1).astype(I32)
    n_used = (ends[-1:] // SLOT_BLOCK).astype(I32)
    trailing = n_used[0] + jnp.arange(nb - t * TOP_K // SLOT_BLOCK, dtype=I32)
    zslot = jnp.concatenate([jnp.where(padded > 0, ends - SLOT_BLOCK, -1),
                             jnp.where(trailing < nb, trailing * SLOT_BLOCK, -1)]).astype(I32)
    dest_flat = _dest(pstart, idx, rank).T.reshape(t * TOP_K)
    xb = _dispatch(zslot, h, dest_flat, n_slots)
    ys = _experts(layer, block_e, n_used, xb, w1, b1, w2, b2)
    out = _combine(x2, dest_flat, gate.T, gate2, final_g, ys, s, final_norm)
    return out.reshape(bsz, s, d)


def kernel(x, c, ada_w, ada_b, mix_norm_g, ffn_norm_g, conf_pw1_w, conf_pw1_b, conf_dw_w, conf_dw_b, conf_ln_g, conf_ln_b, conf_pw2_w, conf_pw2_b, ssm_in_w, ssm_conv_w, ssm_conv_b, ssm_dt_bias, ssm_a_log, ssm_d, ssm_norm_g, ssm_out_w, router_w, router_b, exp_w1, exp_b1, exp_w2, exp_b2, final_norm_g):
    depth = ada_w.shape[0]
    bsz, s, d = x.shape
    mod = _ada(c, ada_w, ada_b).reshape(depth, bsz, 6, 1, d)
    for i in range(depth):
        sh1, sc1, g1, sh2, sc2, g2 = (mod[i, :, m] for m in range(6))
        j = i // 2
        if i % 2 == 0:
            x = _conformer(x, sh1, sc1, g1, mix_norm_g[i], conf_pw1_w[j], conf_pw1_b[j], conf_dw_w[j], conf_dw_b[j],
                           conf_ln_g[j], conf_ln_b[j], conf_pw2_w[j], conf_pw2_b[j])
        else:
            z, xs, bm, cm, dt = _ssm_in(x, sh1, sc1, mix_norm_g[i], ssm_in_w[j], ssm_conv_w[j], ssm_conv_b[j],
                                        ssm_dt_bias[j])
            y = _ssd(xs, bm, cm, dt, z, ssm_a_log[j], ssm_d[j], ssm_norm_g[j])
            x = _out_proj(y, x, g1, ssm_out_w[j])
        x = _moe(i, x, sh2, sc2, g2, ffn_norm_g[i], router_w[i], router_b[i], exp_w1, exp_b1, exp_w2, exp_b2,
                 final_norm_g, final_norm=(i == depth - 1))
    return x
```

```python
import functools

import jax
import jax.numpy as jnp
from jax import lax
from jax.experimental import pallas as pl
from jax.experimental.pallas import tpu as pltpu

F32 = jnp.float32
BF16 = jnp.bfloat16
I32 = jnp.int32

EPS = 1e-5
CONV_WIDTH = 31
SSM_HEAD_DIM = 64
SSM_GROUPS = 8
SSM_STATE = 128
SSM_CONV = 4
SSD_CHUNK = 128
N_EXPERTS = 32
TOP_K = 4
SWIGLU_LIMIT = 7.0
SWIGLU_ALPHA = 1.702

LANES = 128
SUBLANES = 8
TILE_CH = LANES * SUBLANES
SLOT_BLOCK = 256
TOKEN_TILE = 256
SEQ_TILE = 256
HALO = 32
SSM_HALO = 8
CONV_GROUP = 8
VMEM_LIMIT = 56 * 1024 * 1024

HIGHEST = lax.Precision.HIGHEST


def _params(n_axes, vmem=VMEM_LIMIT):
    return pltpu.CompilerParams(dimension_semantics=("arbitrary",) * n_axes, vmem_limit_bytes=vmem)


def _sigmoid(v):
    return 1.0 / (1.0 + jnp.exp(-v))


def _silu(v):
    return v * _sigmoid(v)


def _rms_modulate(x, g, shift, scale):
    ms = jnp.mean(x * x, axis=-1, keepdims=True)
    return (x * lax.rsqrt(ms + EPS) * g) * (1.0 + scale) + shift


def _split_bf16(v):
    hi = v.astype(BF16)
    lo = (v - hi.astype(F32)).astype(BF16)
    return hi, lo


def _store_token_major(ref, tok0, v):
    n, d = v.shape
    assert d == TILE_CH and n % SUBLANES == 0
    for th in range(n // SUBLANES):
        for j in range(SUBLANES):
            ref[pl.ds((tok0 + th * SUBLANES) * SUBLANES + j, SUBLANES, stride=SUBLANES), :] = (
                v[th * SUBLANES:(th + 1) * SUBLANES, j * LANES:(j + 1) * LANES])


def _load_token_major(ref, tok0, n):
    rows = []
    for th in range(n // SUBLANES):
        rows.append(jnp.concatenate(
            [ref[pl.ds((tok0 + th * SUBLANES) * SUBLANES + j, SUBLANES, stride=SUBLANES), :] for j in range(SUBLANES)],
            axis=1))
    return jnp.concatenate(rows, axis=0)


def _tile_rows(tok):
    return pl.ds(pl.multiple_of(tok * SUBLANES, SUBLANES), SUBLANES)


def _token_major_conv(src, dst, w_ref, b_tile, n_tokens, width, halo, post):
    base = halo - (width - 1)

    def group(gi, carry):
        t0 = gi * CONV_GROUP
        tiles = [src[_tile_rows(t0 + base + i), :] for i in range(width + CONV_GROUP - 1)]
        for o in range(CONV_GROUP):
            acc = [b_tile, None]
            for k in range(width):
                term = w_ref[k * SUBLANES:(k + 1) * SUBLANES, :] * tiles[o + k]
                acc[k % 2] = term if acc[k % 2] is None else acc[k % 2] + term
            dst[_tile_rows(t0 + o), :] = post(acc[0] + acc[1])
        return carry

    lax.fori_loop(0, n_tokens // CONV_GROUP, group, 0)


def _ada_kernel(c_ref, w_ref, b_ref, o_ref):
    c = c_ref[...]
    o_ref[0] = jnp.dot(_silu(c), w_ref[0], precision=HIGHEST, preferred_element_type=F32) + b_ref[0]


def _ada(c, ada_w, ada_b):
    depth, d, n = ada_w.shape
    bsz = c.shape[0]
    tn = 1536
    return pl.pallas_call(
        _ada_kernel,
        grid=(depth, n // tn),
        in_specs=[pl.BlockSpec((bsz, d), lambda l, j: (0, 0)),
                  pl.BlockSpec((1, d, tn), lambda l, j: (l, 0, j)),
                  pl.BlockSpec((1, 1, tn), lambda l, j: (l, 0, j))],
        out_specs=pl.BlockSpec((1, bsz, tn), lambda l, j: (l, 0, j)),
        out_shape=jax.ShapeDtypeStruct((depth, bsz, n), F32),
        compiler_params=_params(2),
        name="ada",
    )(c, ada_w, ada_b.reshape(depth, 1, n))


def _conformer_kernel(x_ref, sh_ref, sc_ref, gate_ref, ng_ref, w1_ref, b1_ref, dw_ref, dwb_ref,
                      lng_ref, lnb_ref, w2_ref, b2_ref, o_ref, ubuf, cbuf):
    ts, d = x_ref.shape[1], x_ref.shape[2]
    x = x_ref[0]
    h = _rms_modulate(x, ng_ref[...], sh_ref[0], sc_ref[0])
    u = jnp.dot(h.astype(BF16), w1_ref[...], preferred_element_type=F32) + b1_ref[...]
    u = u[:, :d] * _sigmoid(u[:, d:])

    @pl.when(pl.program_id(1) == 0)
    def _():
        ubuf[0:HALO * SUBLANES, :] = jnp.zeros((HALO * SUBLANES, LANES), F32)

    _store_token_major(ubuf, HALO, u)
    _token_major_conv(ubuf, cbuf, dw_ref, dwb_ref[...], ts, CONV_WIDTH, HALO, lambda a: a)
    ubuf[0:HALO * SUBLANES, :] = ubuf[ts * SUBLANES:(ts + HALO) * SUBLANES, :]

    v = _load_token_major(cbuf, 0, ts)
    mu = jnp.mean(v, axis=-1, keepdims=True)
    vc = v - mu
    var = jnp.mean(vc * vc, axis=-1, keepdims=True)
    v = _silu(vc * lax.rsqrt(var + EPS) * lng_ref[...] + lnb_ref[...])
    y = jnp.dot(v.astype(BF16), w2_ref[...], preferred_element_type=F32) + b2_ref[...]
    o_ref[0] = x + gate_ref[0] * y


def _conformer(x, shift, scale, gate, norm_g, pw1_w, pw1_b, dw_w, dw_b, ln_g, ln_b, pw2_w, pw2_b):
    bsz, s, d = x.shape
    assert d == TILE_CH
    ts = SEQ_TILE
    row = lambda a: a.reshape(1, -1)
    per_batch = pl.BlockSpec((1, 1, d), lambda b, j: (b, 0, 0))
    full = lambda a: pl.BlockSpec(a.shape, lambda b, j: (0,) * a.ndim)
    consts = [row(norm_g), pw1_w.astype(BF16), row(pw1_b), dw_w.reshape(CONV_WIDTH * SUBLANES, LANES),
              dw_b.reshape(SUBLANES, LANES), row(ln_g), row(ln_b), pw2_w.astype(BF16), row(pw2_b)]
    return pl.pallas_call(
        _conformer_kernel,
        grid=(bsz, s // ts),
        in_specs=[pl.BlockSpec((1, ts, d), lambda b, j: (b, j, 0)), per_batch, per_batch, per_batch]
                 + [full(a) for a in consts],
        out_specs=pl.BlockSpec((1, ts, d), lambda b, j: (b, j, 0)),
        out_shape=jax.ShapeDtypeStruct((bsz, s, d), F32),
        scratch_shapes=[pltpu.VMEM(((HALO + ts) * SUBLANES, LANES), F32), pltpu.VMEM((ts * SUBLANES, LANES), F32)],
        compiler_params=_params(2),
        name="conformer",
    )(x, shift, scale, gate, *consts)


def _ssm_in_kernel(x_ref, sh_ref, sc_ref, ng_ref, wz_ref, wx_ref, wdt_ref, cw_ref, cb_ref, dtb_ref,
                   z_ref, xs_ref, b_ref, c_ref, dt_ref, buf, cbuf):
    ts = x_ref.shape[1]
    d_inner = xs_ref.shape[2]
    gn = b_ref.shape[2]
    n_col = buf.shape[0]
    nh = d_inner // SSM_HEAD_DIM
    h = _rms_modulate(x_ref[0], ng_ref[...], sh_ref[0], sc_ref[0]).astype(BF16)
    z_ref[0] = jnp.dot(h, wz_ref[...], preferred_element_type=F32).astype(BF16)
    dt = jnp.dot(h, wdt_ref[...], preferred_element_type=F32) + dtb_ref[...]
    dt = jnp.maximum(dt, 0.0) + jnp.log(1.0 + jnp.exp(-jnp.abs(dt)))
    dt_ref[0] = jnp.where(lax.broadcasted_iota(I32, dt.shape, 1) < nh, dt, 0.0)

    @pl.when(pl.program_id(1) == 0)
    def _():
        buf[:, 0:SSM_HALO * SUBLANES, :] = jnp.zeros((n_col, SSM_HALO * SUBLANES, LANES), F32)

    outs = []
    for q in range(d_inner // TILE_CH):
        outs.append((xs_ref, q * TILE_CH))
    for q in range(gn // TILE_CH):
        outs.append((b_ref, q * TILE_CH))
    for q in range(gn // TILE_CH):
        outs.append((c_ref, q * TILE_CH))
    for q in range(n_col):
        xq = jnp.dot(h, wx_ref[:, q * TILE_CH:(q + 1) * TILE_CH], preferred_element_type=F32)
        _store_token_major(buf.at[q], SSM_HALO, xq)
        _token_major_conv(buf.at[q], cbuf, cw_ref.at[q], cb_ref[q], ts, SSM_CONV, SSM_HALO, _silu)
        buf[q, 0:SSM_HALO * SUBLANES, :] = buf[q, ts * SUBLANES:(ts + SSM_HALO) * SUBLANES, :]
        ref, lo = outs[q]
        ref[0, :, lo:lo + TILE_CH] = _load_token_major(cbuf, 0, ts).astype(BF16)


def _ssm_in(x, shift, scale, norm_g, in_w, conv_w, conv_b, dt_bias):
    bsz, s, d = x.shape
    ts = SEQ_TILE
    nh = dt_bias.shape[0]
    d_inner = nh * SSM_HEAD_DIM
    gn = SSM_GROUPS * SSM_STATE
    conv_dim = d_inner + 2 * gn
    n_col = conv_dim // TILE_CH
    assert d_inner % TILE_CH == 0 and gn % TILE_CH == 0
    w = in_w.astype(BF16)
    lane_pad = lambda a: jnp.pad(a, ((0, 0), (0, LANES - nh)))
    cw = conv_w.reshape(SSM_CONV, n_col, SUBLANES, LANES).transpose(1, 0, 2, 3).reshape(n_col, SSM_CONV * SUBLANES, LANES)
    cb = conv_b.reshape(n_col, SUBLANES, LANES)
    consts = [norm_g.reshape(1, d), w[:, :d_inner], w[:, d_inner:d_inner + conv_dim],
              lane_pad(w[:, d_inner + conv_dim:]), cw, cb, lane_pad(dt_bias.reshape(1, nh))]
    per_batch = pl.BlockSpec((1, 1, d), lambda b, j: (b, 0, 0))
    full = lambda a: pl.BlockSpec(a.shape, lambda b, j: (0,) * a.ndim)
    tile = lambda n: pl.BlockSpec((1, ts, n), lambda b, j: (b, j, 0))
    return pl.pallas_call(
        _ssm_in_kernel,
        grid=(bsz, s // ts),
        in_specs=[tile(d), per_batch, per_batch] + [full(a) for a in consts],
        out_specs=[tile(d_inner), tile(d_inner), tile(gn), tile(gn), tile(LANES)],
        out_shape=[jax.ShapeDtypeStruct((bsz, s, d_inner), BF16), jax.ShapeDtypeStruct((bsz, s, d_inner), BF16),
                   jax.ShapeDtypeStruct((bsz, s, gn), BF16), jax.ShapeDtypeStruct((bsz, s, gn), BF16),
                   jax.ShapeDtypeStruct((bsz, s, LANES), F32)],
        scratch_shapes=[pltpu.VMEM((n_col, (SSM_HALO + ts) * SUBLANES, LANES), F32),
                        pltpu.VMEM((ts * SUBLANES, LANES), F32)],
        compiler_params=_params(2),
        name="ssm_in",
    )(x, shift, scale, *consts)


def _expand_heads(v, rep):
    hi, lo = _split_bf16(v)
    return (jnp.dot(hi, rep, preferred_element_type=F32) + jnp.dot(lo, rep, preferred_element_type=F32))


def _ssd_kernel(xs_ref, b_ref, c_ref, dt_ref, z_ref, alog_ref, dskip_ref, ng_ref, o_ref, state):
    L = xs_ref.shape[1]
    d_inner = xs_ref.shape[2]
    hp = dt_ref.shape[2]
    hg = d_inner // SSM_HEAD_DIM // SSM_GROUPS
    gw = hg * SSM_HEAD_DIM
    n = SSM_STATE

    @pl.when(pl.program_id(1) == 0)
    def _():
        state[...] = jnp.zeros(state.shape, F32)

    dt = dt_ref[0]
    a = dt * (-jnp.exp(alog_ref[...]))
    row = lax.broadcasted_iota(I32, (L, L), 0)
    col = lax.broadcasted_iota(I32, (L, L), 1)
    causal = row >= col
    tri = causal.astype(F32)
    cum = jnp.dot(tri, a, precision=HIGHEST, preferred_element_type=F32)
    cum_t = cum.T
    dt_t = dt.T
    total = cum[L - 1:L, :]

    head_of = lax.broadcasted_iota(I32, (hp, d_inner), 1) // SSM_HEAD_DIM
    rep = (head_of == lax.broadcasted_iota(I32, (hp, d_inner), 0)).astype(BF16)
    grow = _expand_heads(jnp.exp(cum), rep)
    to_end = _expand_heads(jnp.exp(total - cum) * dt, rep)
    keep = _expand_heads(jnp.exp(total), rep)
    head_in_group = lax.broadcasted_iota(I32, (L, gw), 1) // SSM_HEAD_DIM

    xs = xs_ref[0]
    xf = xs.astype(F32)
    for g in range(SSM_GROUPS):
        bg = b_ref[0, :, g * n:(g + 1) * n]
        cg = c_ref[0, :, g * n:(g + 1) * n]
        xg = xs[:, g * gw:(g + 1) * gw]
        cb = lax.dot_general(cg, bg, (((1,), (1,)), ((), ())), preferred_element_type=F32)
        st = state[g]
        y_off = jnp.dot(cg, st.astype(BF16), preferred_element_type=F32)
        y_diag = jnp.zeros((L, gw), F32)
        for hh in range(hg):
            h = g * hg + hh
            seg = cum[:, h:h + 1] - cum_t[h:h + 1, :]
            w = jnp.where(causal, jnp.exp(seg), 0.0) * cb * dt_t[h:h + 1, :]
            x_head = jnp.where(head_in_group == hh, xg, jnp.zeros_like(xg))
            y_diag = y_diag + jnp.dot(w.astype(BF16), x_head, preferred_element_type=F32)
        sl = slice(g * gw, (g + 1) * gw)
        y = y_diag + y_off * grow[:, sl] + dskip_ref[:, sl] * xf[:, sl]
        xw = (xf[:, sl] * to_end[:, sl]).astype(BF16)
        upd = lax.dot_general(bg, xw, (((0,), (0,)), ((), ())), preferred_element_type=F32)
        state[g] = st * keep[:, sl] + upd
        y = y * _silu(z_ref[0, :, sl].astype(F32))
        y = y * lax.rsqrt(jnp.mean(y * y, axis=-1, keepdims=True) + EPS)
        o_ref[0, :, sl] = (y * ng_ref[:, sl]).astype(BF16)


def _ssd(xs, bm, cm, dt, z, a_log, d_skip, norm_g):
    bsz, s, d_inner = xs.shape
    nh = a_log.shape[0]
    hp = dt.shape[2]
    gn = bm.shape[2]
    L = SSD_CHUNK
    gw = d_inner // SSM_GROUPS
    consts = [jnp.pad(a_log.reshape(1, nh), ((0, 0), (0, hp - nh))),
              jnp.repeat(d_skip, SSM_HEAD_DIM).reshape(1, d_inner), norm_g.reshape(1, d_inner)]
    tile = lambda n: pl.BlockSpec((1, L, n), lambda b, j: (b, j, 0))
    full = lambda a: pl.BlockSpec(a.shape, lambda b, j: (0,) * a.ndim)
    return pl.pallas_call(
        _ssd_kernel,
        grid=(bsz, s // L),
        in_specs=[tile(d_inner), tile(gn), tile(gn), tile(hp), tile(d_inner)] + [full(a) for a in consts],
        out_specs=tile(d_inner),
        out_shape=jax.ShapeDtypeStruct((bsz, s, d_inner), BF16),
        scratch_shapes=[pltpu.VMEM((SSM_GROUPS, SSM_STATE, gw), F32)],
        compiler_params=_params(2),
        name="ssd",
    )(xs, bm, cm, dt, z, *consts)


def _out_proj_kernel(y_ref, x_ref, gate_ref, w_ref, o_ref):
    o_ref[0] = x_ref[0] + gate_ref[0] * jnp.dot(y_ref[0], w_ref[...], preferred_element_type=F32)


def _out_proj(y, x, gate, out_w):
    bsz, s, d = x.shape
    k = y.shape[2]
    ts = 512
    return pl.pallas_call(
        _out_proj_kernel,
        grid=(bsz, s // ts),
        in_specs=[pl.BlockSpec((1, ts, k), lambda b, j: (b, j, 0)), pl.BlockSpec((1, ts, d), lambda b, j: (b, j, 0)),
                  pl.BlockSpec((1, 1, d), lambda b, j: (b, 0, 0)), pl.BlockSpec((k, d), lambda b, j: (0, 0))],
        out_specs=pl.BlockSpec((1, ts, d), lambda b, j: (b, j, 0)),
        out_shape=jax.ShapeDtypeStruct((bsz, s, d), F32),
        compiler_params=_params(2),
        name="out_proj",
    )(y, x, gate, out_w.astype(BF16))


def _route_kernel(x_ref, sh_ref, sc_ref, ng_ref, wt_ref, rb_ref, h_ref, gate_ref, lpos_ref, info_ref, cnt_ref, running):
    tm = x_ref.shape[0]
    ne = wt_ref.shape[0]

    @pl.when(pl.program_id(0) == 0)
    def _():
        running[...] = jnp.zeros(running.shape, F32)

    h = _rms_modulate(x_ref[...], ng_ref[...], sh_ref[0], sc_ref[0])
    _store_token_major(h_ref, 0, h)
    h_hi, h_lo = _split_bf16(h)
    w_hi, w_lo = _split_bf16(wt_ref[...])
    nt = (((1,), (1,)), ((), ()))
    logits = (lax.dot_general(w_hi, h_hi, nt, preferred_element_type=F32)
              + lax.dot_general(w_hi, h_lo, nt, preferred_element_type=F32)
              + lax.dot_general(w_lo, h_hi, nt, preferred_element_type=F32)) + rb_ref[...]

    e_iota = lax.broadcasted_iota(I32, (ne, tm), 0).astype(F32)
    vals = logits
    tops, sels = [], []
    for _ in range(TOP_K):
        m = jnp.max(vals, axis=0, keepdims=True)
        idx = jnp.min(jnp.where(vals == m, e_iota, float(ne)), axis=0, keepdims=True)
        sel = e_iota == idx
        vals = jnp.where(sel, -jnp.inf, vals)
        tops.append(m)
        sels.append(sel)
    exps = [jnp.exp(t - tops[0]) for t in tops]
    inv = 1.0 / (exps[0] + exps[1] + exps[2] + exps[3])

    member = jnp.zeros((ne, tm), F32)
    for sel in sels:
        member = member + sel.astype(F32)
    before = (lax.broadcasted_iota(I32, (tm, tm), 0) < lax.broadcasted_iota(I32, (tm, tm), 1)).astype(BF16)
    earlier = jnp.dot(member.astype(BF16), before, preferred_element_type=F32)
    cnt = jnp.sum(member, axis=1, keepdims=True)
    lower = (lax.broadcasted_iota(I32, (ne, ne), 0) > lax.broadcasted_iota(I32, (ne, ne), 1)).astype(BF16)
    off = jnp.dot(lower, jnp.broadcast_to(cnt, (ne, LANES)).astype(BF16), preferred_element_type=F32)[:, 0:1]
    for k in range(TOP_K):
        gate_ref[k:k + 1, :] = exps[k] * inv
        lpos_ref[k:k + 1, :] = jnp.sum(jnp.where(sels[k], earlier + off, 0.0), axis=0, keepdims=True).astype(I32)
    lane = lax.broadcasted_iota(I32, (ne, LANES), 1)
    info = jnp.where(lane == 0, cnt, jnp.where(lane == 1, running[...], jnp.where(lane == 2, off, 0.0)))
    info_ref[...] = info.astype(I32)
    running[...] = running[...] + cnt
    cnt_ref[...] = jnp.broadcast_to(running[...], cnt_ref.shape).astype(I32)


def _route(x2, shift, scale, norm_g, router_w, router_b, tokens_per_batch):
    t, d = x2.shape
    assert d == TILE_CH
    ne = router_w.shape[1]
    tm = TOKEN_TILE
    tiles_per_batch = tokens_per_batch // tm
    per_batch = pl.BlockSpec((1, 1, d), lambda i: (i // tiles_per_batch, 0, 0))
    full = lambda shape: pl.BlockSpec(shape, lambda i: (0,) * len(shape))
    kt = pl.BlockSpec((TOP_K, tm), lambda i: (0, i))
    return pl.pallas_call(
        _route_kernel,
        grid=(t // tm,),
        in_specs=[pl.BlockSpec((tm, d), lambda i: (i, 0)), per_batch, per_batch,
                  full((1, d)), full((ne, d)), full((ne, 1))],
        out_specs=[pl.BlockSpec((tm * SUBLANES, LANES), lambda i: (i, 0)), kt, kt,
                   pl.BlockSpec((ne, LANES), lambda i: (i, 0)), full((ne, LANES))],
        out_shape=[jax.ShapeDtypeStruct((t * SUBLANES, LANES), F32), jax.ShapeDtypeStruct((TOP_K, t), F32),
                   jax.ShapeDtypeStruct((TOP_K, t), I32), jax.ShapeDtypeStruct((t // tm * ne, LANES), I32),
                   jax.ShapeDtypeStruct((ne, LANES), I32)],
        scratch_shapes=[pltpu.VMEM((ne, 1), F32)],
        compiler_params=_params(1),
        name="route",
    )(x2, shift, scale, norm_g.reshape(1, d), router_w.T, router_b.reshape(ne, 1))


RUN_PIECES = tuple(1 << b for b in range(TOKEN_TILE.bit_length() - 1, -1, -1))


def _run_copies(n, src_ref, src_tok, dst_ref, dst_tok, sem, fn):
    for b in RUN_PIECES:
        done = n & ~(2 * b - 1)

        @pl.when((n & b) != 0)
        def _():
            src = pl.ds(pl.multiple_of((src_tok + done) * SUBLANES, SUBLANES), b * SUBLANES)
            dst = pl.ds(pl.multiple_of((dst_tok + done) * SUBLANES, SUBLANES), b * SUBLANES)
            fn(pltpu.make_async_copy(src_ref.at[src], dst_ref.at[dst], sem))


def _tile_runs(step, n_experts, cnt_ref, fn):
    def body(e, carry):
        j = step * n_experts + e
        fn(j, cnt_ref[j])
        return carry

    lax.fori_loop(0, n_experts, body, 0)


def _dispatch_kernel(zslot_ref, tslot_ref, tcnt_ref, toff_ref, h_ref, lpos_ref, xb_ref, zeros, stage, sem, zsem):
    tm = h_ref.shape[0] // SUBLANES
    block_rows = SLOT_BLOCK * SUBLANES
    ne = N_EXPERTS
    i = pl.program_id(0)
    last = pl.num_programs(0) - 1

    @pl.when(pl.program_id(0) == 0)
    def _():
        zeros[...] = jnp.zeros(zeros.shape, F32)

        def zcopy(e):
            rows = pl.ds(pl.multiple_of(zslot_ref[e] * SUBLANES, block_rows), block_rows)
            return pltpu.make_async_copy(zeros, xb_ref.at[rows], zsem)

        def zstart(e, carry):
            @pl.when(zslot_ref[e] >= 0)
            def _():
                zcopy(e).start()
            return carry

        def zwait(e, carry):
            @pl.when(zslot_ref[e] >= 0)
            def _():
                zcopy(e).wait()
            return carry

        lax.fori_loop(0, zslot_ref.shape[0], zstart, 0)
        lax.fori_loop(0, zslot_ref.shape[0], zwait, 0)

    buf = stage.at[i % 2]

    def place(t, carry):
        tile = h_ref[_tile_rows(t), :]
        for k in range(TOP_K):
            buf[_tile_rows(lpos_ref[t * TOP_K + k]), :] = tile
        return carry

    lax.fori_loop(0, tm, place, 0, unroll=4)

    def runs(step, fn):
        src, s = stage.at[step % 2], sem.at[step % 2]
        _tile_runs(step, ne, tcnt_ref,
                   lambda j, n: _run_copies(n, src, toff_ref[j], xb_ref, tslot_ref[j], s, fn))

    runs(i, lambda cp: cp.start())

    @pl.when(i > 0)
    def _():
        runs(i - 1, lambda cp: cp.wait())

    @pl.when(i == last)
    def _():
        runs(i, lambda cp: cp.wait())


def _dispatch(zslot, tslot, tcnt, toff, h, lpos_flat, n_slots):
    t = h.shape[0] // SUBLANES
    tm = TOKEN_TILE
    return pl.pallas_call(
        _dispatch_kernel,
        grid_spec=pltpu.PrefetchScalarGridSpec(
            num_scalar_prefetch=4,
            grid=(t // tm,),
            in_specs=[pl.BlockSpec((tm * SUBLANES, LANES), lambda i, *_: (i, 0)),
                      pl.BlockSpec((tm * TOP_K,), lambda i, *_: (i,), memory_space=pltpu.SMEM)],
            out_specs=pl.BlockSpec(memory_space=pl.ANY),
            scratch_shapes=[pltpu.VMEM((SLOT_BLOCK * SUBLANES, LANES), F32),
                            pltpu.VMEM((2, tm * TOP_K * SUBLANES, LANES), F32),
                            pltpu.SemaphoreType.DMA((2,)), pltpu.SemaphoreType.DMA(())],
        ),
        out_shape=jax.ShapeDtypeStruct((n_slots * SUBLANES, LANES), F32),
        compiler_params=_params(1),
        name="dispatch",
    )(zslot, tslot, tcnt, toff, h, lpos_flat)


def _experts_kernel(be_ref, nu_ref, x_ref, w1_ref, b1_ref, w2_ref, b2_ref, y_ref, w1b, w2b):
    i = pl.program_id(0)
    f = w2b.shape[0]
    used = i < nu_ref[0]
    changed = (i == 0) | (be_ref[i] != be_ref[jnp.maximum(i - 1, 0)])

    @pl.when(used & changed)
    def _():
        w1b[...] = w1_ref[0, 0].astype(BF16)
        w2b[...] = w2_ref[0, 0].astype(BF16)

    @pl.when(jnp.logical_not(used))
    def _():
        y_ref[...] = jnp.zeros(y_ref.shape, F32)

    @pl.when(used)
    def _():
        x = _load_token_major(x_ref, 0, SLOT_BLOCK).astype(BF16)
        gu = jnp.dot(x, w1b[...], preferred_element_type=F32) + b1_ref[0, 0]
        g = jnp.minimum(gu[:, :f], SWIGLU_LIMIT)
        u = jnp.clip(gu[:, f:], -SWIGLU_LIMIT, SWIGLU_LIMIT)
        act = (u + 1.0) * (g * _sigmoid(SWIGLU_ALPHA * g))
        y = jnp.dot(act.astype(BF16), w2b[...], preferred_element_type=F32) + b2_ref[0, 0]
        _store_token_major(y_ref, 0, y)


def _experts(layer, block_e, n_used, xb, w1, b1, w2, b2):
    n_slots = xb.shape[0] // SUBLANES
    _, ne, d, f2 = w1.shape
    f = w2.shape[2]
    assert d == TILE_CH
    nb = n_slots // SLOT_BLOCK
    blk = lambda i, be, nu: jnp.minimum(i, nu[0] - 1)
    rows = (SLOT_BLOCK * SUBLANES, LANES)
    per_e = lambda shape: pl.BlockSpec((1, 1) + shape, lambda i, be, nu: (layer, be[blk(i, be, nu)], 0, 0))
    return pl.pallas_call(
        _experts_kernel,
        grid_spec=pltpu.PrefetchScalarGridSpec(
            num_scalar_prefetch=2,
            grid=(nb,),
            in_specs=[pl.BlockSpec(rows, lambda i, be, nu: (blk(i, be, nu), 0)),
                      per_e((d, f2)), per_e((1, f2)), per_e((f, d)), per_e((1, d))],
            out_specs=pl.BlockSpec(rows, lambda i, be, nu: (i, 0)),
            scratch_shapes=[pltpu.VMEM((d, f2), BF16), pltpu.VMEM((f, d), BF16)],
        ),
        out_shape=jax.ShapeDtypeStruct((n_slots * SUBLANES, LANES), F32),
        compiler_params=_params(1),
        name="experts",
    )(block_e, n_used, xb, w1, b1.reshape(b1.shape[0], ne, 1, f2), w2, b2.reshape(b2.shape[0], ne, 1, d))


def _combine_kernel(final_norm, tslot_ref, tcnt_ref, toff_ref, x_ref, lpos_ref, gate_ref, g2_ref, fg_ref, ys_ref,
                    o_ref, stage, obuf, sem):
    tm = x_ref.shape[0]
    ne = N_EXPERTS
    i = pl.program_id(0)
    last = pl.num_programs(0) - 1

    def runs(step, fn):
        dst, s = stage.at[step % 2], sem.at[step % 2]
        _tile_runs(step, ne, tcnt_ref,
                   lambda j, n: _run_copies(n, ys_ref, tslot_ref[j], dst, toff_ref[j], s, fn))

    @pl.when(i == 0)
    def _():
        runs(i, lambda cp: cp.start())

    @pl.when(i < last)
    def _():
        runs(i + 1, lambda cp: cp.start())

    runs(i, lambda cp: cp.wait())
    buf = stage.at[i % 2]

    def mix(t, carry):
        acc = gate_ref[t * TOP_K] * buf[_tile_rows(lpos_ref[t * TOP_K]), :]
        for k in range(1, TOP_K):
            acc = acc + gate_ref[t * TOP_K + k] * buf[_tile_rows(lpos_ref[t * TOP_K + k]), :]
        obuf[_tile_rows(t), :] = acc
        return carry

    lax.fori_loop(0, tm, mix, 0, unroll=4)
    x = x_ref[...] + g2_ref[0] * _load_token_major(obuf, 0, tm)
    if final_norm:
        x = x * lax.rsqrt(jnp.mean(x * x, axis=-1, keepdims=True) + EPS) * fg_ref[...]
    o_ref[...] = x


def _combine(tslot, tcnt, toff, x2, lpos_flat, gate_flat, g2, final_g, ys, tokens_per_batch, final_norm):
    t, d = x2.shape
    tm = TOKEN_TILE
    tiles_per_batch = tokens_per_batch // tm
    pairs = pl.BlockSpec((tm * TOP_K,), lambda i, *_: (i,), memory_space=pltpu.SMEM)
    return pl.pallas_call(
        functools.partial(_combine_kernel, final_norm),
        grid_spec=pltpu.PrefetchScalarGridSpec(
            num_scalar_prefetch=3,
            grid=(t // tm,),
            in_specs=[pl.BlockSpec((tm, d), lambda i, *_: (i, 0)), pairs, pairs,
                      pl.BlockSpec((1, 1, d), lambda i, *_: (i // tiles_per_batch, 0, 0)),
                      pl.BlockSpec((1, d), lambda i, *_: (0, 0)),
                      pl.BlockSpec(memory_space=pl.ANY)],
            out_specs=pl.BlockSpec((tm, d), lambda i, *_: (i, 0)),
            scratch_shapes=[pltpu.VMEM((2, tm * TOP_K * SUBLANES, LANES), F32), pltpu.VMEM((tm * SUBLANES, LANES), F32),
                            pltpu.SemaphoreType.DMA((2,))],
        ),
        out_shape=jax.ShapeDtypeStruct((t, d), F32),
        compiler_params=_params(1),
        name="combine",
    )(tslot, tcnt, toff, x2, lpos_flat, gate_flat, g2, final_g.reshape(1, d), ys)


def _moe(layer, x, shift, scale, gate2, norm_g, router_w, router_b, w1, b1, w2, b2, final_g, final_norm):
    bsz, s, d = x.shape
    t = bsz * s
    ne = router_w.shape[1]
    x2 = x.reshape(t, d)
    h, gate, lpos, info, counts = _route(x2, shift, scale, norm_g, router_w, router_b, s)
    counts = counts[:, 0]
    padded = ((counts + SLOT_BLOCK - 1) // SLOT_BLOCK) * SLOT_BLOCK
    ends = jnp.cumsum(padded)
    pstart = (ends - padded).astype(I32)
    n_slots = t * TOP_K + ne * SLOT_BLOCK
    nb = n_slots // SLOT_BLOCK
    block_start = jnp.arange(nb, dtype=I32) * SLOT_BLOCK
    block_e = jnp.minimum(jnp.sum(block_start[:, None] >= ends[None, :], axis=1), ne - 1).astype(I32)
    n_used = (ends[-1:] // SLOT_BLOCK).astype(I32)
    trailing = n_used[0] + jnp.arange(nb - t * TOP_K // SLOT_BLOCK, dtype=I32)
    zslot = jnp.concatenate([jnp.where(padded > 0, ends - SLOT_BLOCK, -1),
                             jnp.where(trailing < nb, trailing * SLOT_BLOCK, -1)]).astype(I32)
    tcnt, tbase, toff = info[:, 0], info[:, 1], info[:, 2]
    tslot = tbase + jnp.tile(pstart, t // TOKEN_TILE)
    pair_major = lambda a: a.T.reshape(t * TOP_K)
    lpos_flat = pair_major(lpos)
    xb = _dispatch(zslot, tslot, tcnt, toff, h, lpos_flat, n_slots)
    ys = _experts(layer, block_e, n_used, xb, w1, b1, w2, b2)
    out = _combine(tslot, tcnt, toff, x2, lpos_flat, pair_major(gate), gate2, final_g, ys, s, final_norm)
    return out.reshape(bsz, s, d)


def kernel(x, c, ada_w, ada_b, mix_norm_g, ffn_norm_g, conf_pw1_w, conf_pw1_b, conf_dw_w, conf_dw_b, conf_ln_g, conf_ln_b, conf_pw2_w, conf_pw2_b, ssm_in_w, ssm_conv_w, ssm_conv_b, ssm_dt_bias, ssm_a_log, ssm_d, ssm_norm_g, ssm_out_w, router_w, router_b, exp_w1, exp_b1, exp_w2, exp_b2, final_norm_g):
    depth = ada_w.shape[0]
    bsz, s, d = x.shape
    mod = _ada(c, ada_w, ada_b).reshape(depth, bsz, 6, 1, d)
    for i in range(depth):
        sh1, sc1, g1, sh2, sc2, g2 = (mod[i, :, m] for m in range(6))
        j = i // 2
        if i % 2 == 0:
            x = _conformer(x, sh1, sc1, g1, mix_norm_g[i], conf_pw1_w[j], conf_pw1_b[j], conf_dw_w[j], conf_dw_b[j],
                           conf_ln_g[j], conf_ln_b[j], conf_pw2_w[j], conf_pw2_b[j])
        else:
            z, xs, bm, cm, dt = _ssm_in(x, sh1, sc1, mix_norm_g[i], ssm_in_w[j], ssm_conv_w[j], ssm_conv_b[j],
                                        ssm_dt_bias[j])
            y = _ssd(xs, bm, cm, dt, z, ssm_a_log[j], ssm_d[j], ssm_norm_g[j])
            x = _out_proj(y, x, g1, ssm_out_w[j])
        x = _moe(i, x, sh2, sc2, g2, ffn_norm_g[i], router_w[i], router_b[i], exp_w1, exp_b1, exp_w2, exp_b2,
                 final_norm_g, final_norm=(i == depth - 1))
    return x
```

```python
import functools

import jax
import jax.numpy as jnp
from jax import lax
from jax.experimental import pallas as pl
from jax.experimental.pallas import tpu as pltpu

F32 = jnp.float32
BF16 = jnp.bfloat16
I32 = jnp.int32

EPS = 1e-5
CONV_WIDTH = 31
SSM_HEAD_DIM = 64
SSM_GROUPS = 8
SSM_STATE = 128
SSM_CONV = 4
SSD_CHUNK = 128
N_EXPERTS = 32
TOP_K = 4
SWIGLU_LIMIT = 7.0
SWIGLU_ALPHA = 1.702

LANES = 128
SUBLANES = 8
TILE_CH = LANES * SUBLANES
SLOT_BLOCK = 512
TOKEN_TILE = 256
SEQ_TILE = 256
HALO = 32
SSM_HALO = 8
CONV_GROUP = 8
VMEM_LIMIT = 56 * 1024 * 1024

HIGHEST = lax.Precision.HIGHEST


def _params(n_axes, vmem=VMEM_LIMIT):
    return pltpu.CompilerParams(dimension_semantics=("arbitrary",) * n_axes, vmem_limit_bytes=vmem)


def _sigmoid(v):
    return 1.0 / (1.0 + jnp.exp(-v))


def _silu(v):
    return v * _sigmoid(v)


def _rms_modulate(x, g, shift, scale):
    ms = jnp.mean(x * x, axis=-1, keepdims=True)
    return (x * lax.rsqrt(ms + EPS) * g) * (1.0 + scale) + shift


def _split_bf16(v):
    hi = v.astype(BF16)
    lo = (v - hi.astype(F32)).astype(BF16)
    return hi, lo


def _store_token_major(ref, tok0, v):
    n, d = v.shape
    assert d == TILE_CH and n % SUBLANES == 0
    for th in range(n // SUBLANES):
        for j in range(SUBLANES):
            ref[pl.ds((tok0 + th * SUBLANES) * SUBLANES + j, SUBLANES, stride=SUBLANES), :] = (
                v[th * SUBLANES:(th + 1) * SUBLANES, j * LANES:(j + 1) * LANES])


def _load_token_major(ref, tok0, n):
    rows = []
    for th in range(n // SUBLANES):
        rows.append(jnp.concatenate(
            [ref[pl.ds((tok0 + th * SUBLANES) * SUBLANES + j, SUBLANES, stride=SUBLANES), :] for j in range(SUBLANES)],
            axis=1))
    return jnp.concatenate(rows, axis=0)


def _tile_rows(tok):
    if isinstance(tok, int):
        return pl.ds(tok * SUBLANES, SUBLANES)
    return pl.ds(pl.multiple_of(tok * SUBLANES, SUBLANES), SUBLANES)


def _token_major_conv(src, dst, w_ref, b_tile, n_tokens, width, halo, post, unrolled=False):
    base = halo - (width - 1)

    def group(gi, carry):
        t0 = gi * CONV_GROUP
        tiles = [src[_tile_rows(t0 + base + i), :] for i in range(width + CONV_GROUP - 1)]
        for o in range(CONV_GROUP):
            acc = [b_tile, None]
            for k in range(width):
                term = w_ref[k * SUBLANES:(k + 1) * SUBLANES, :] * tiles[o + k]
                acc[k % 2] = term if acc[k % 2] is None else acc[k % 2] + term
            dst[_tile_rows(t0 + o), :] = post(acc[0] + acc[1])
        return carry

    if unrolled:
        for gi in range(n_tokens // CONV_GROUP):
            group(gi, 0)
    else:
        lax.fori_loop(0, n_tokens // CONV_GROUP, group, 0)


def _ada_kernel(c_ref, w_ref, b_ref, o_ref):
    c = c_ref[...]
    o_ref[0] = jnp.dot(_silu(c), w_ref[0], precision=HIGHEST, preferred_element_type=F32) + b_ref[0]


def _ada(c, ada_w, ada_b):
    depth, d, n = ada_w.shape
    bsz = c.shape[0]
    tn = 1536
    return pl.pallas_call(
        _ada_kernel,
        grid=(depth, n // tn),
        in_specs=[pl.BlockSpec((bsz, d), lambda l, j: (0, 0)),
                  pl.BlockSpec((1, d, tn), lambda l, j: (l, 0, j)),
                  pl.BlockSpec((1, 1, tn), lambda l, j: (l, 0, j))],
        out_specs=pl.BlockSpec((1, bsz, tn), lambda l, j: (l, 0, j)),
        out_shape=jax.ShapeDtypeStruct((depth, bsz, n), F32),
        compiler_params=_params(2),
        name="ada",
    )(c, ada_w, ada_b.reshape(depth, 1, n))


def _conformer_kernel(x_ref, sh_ref, sc_ref, gate_ref, ng_ref, w1_ref, b1_ref, dw_ref, dwb_ref,
                      lng_ref, lnb_ref, w2_ref, b2_ref, o_ref, ubuf, cbuf):
    ts, d = x_ref.shape[1], x_ref.shape[2]
    x = x_ref[0]
    h = _rms_modulate(x, ng_ref[...], sh_ref[0], sc_ref[0])
    u = jnp.dot(h.astype(BF16), w1_ref[...], preferred_element_type=F32) + b1_ref[...]
    u = u[:, :d] * _sigmoid(u[:, d:])

    @pl.when(pl.program_id(1) == 0)
    def _():
        ubuf[0:HALO * SUBLANES, :] = jnp.zeros((HALO * SUBLANES, LANES), F32)

    _store_token_major(ubuf, HALO, u)
    _token_major_conv(ubuf, cbuf, dw_ref, dwb_ref[...], ts, CONV_WIDTH, HALO, lambda a: a)
    ubuf[0:HALO * SUBLANES, :] = ubuf[ts * SUBLANES:(ts + HALO) * SUBLANES, :]

    v = _load_token_major(cbuf, 0, ts)
    mu = jnp.mean(v, axis=-1, keepdims=True)
    vc = v - mu
    var = jnp.mean(vc * vc, axis=-1, keepdims=True)
    v = _silu(vc * lax.rsqrt(var + EPS) * lng_ref[...] + lnb_ref[...])
    y = jnp.dot(v.astype(BF16), w2_ref[...], preferred_element_type=F32) + b2_ref[...]
    o_ref[0] = x + gate_ref[0] * y


def _conformer(x, shift, scale, gate, norm_g, pw1_w, pw1_b, dw_w, dw_b, ln_g, ln_b, pw2_w, pw2_b):
    bsz, s, d = x.shape
    assert d == TILE_CH
    ts = SEQ_TILE
    row = lambda a: a.reshape(1, -1)
    per_batch = pl.BlockSpec((1, 1, d), lambda b, j: (b, 0, 0))
    full = lambda a: pl.BlockSpec(a.shape, lambda b, j: (0,) * a.ndim)
    consts = [row(norm_g), pw1_w.astype(BF16), row(pw1_b), dw_w.reshape(CONV_WIDTH * SUBLANES, LANES),
              dw_b.reshape(SUBLANES, LANES), row(ln_g), row(ln_b), pw2_w.astype(BF16), row(pw2_b)]
    return pl.pallas_call(
        _conformer_kernel,
        grid=(bsz, s // ts),
        in_specs=[pl.BlockSpec((1, ts, d), lambda b, j: (b, j, 0)), per_batch, per_batch, per_batch]
                 + [full(a) for a in consts],
        out_specs=pl.BlockSpec((1, ts, d), lambda b, j: (b, j, 0)),
        out_shape=jax.ShapeDtypeStruct((bsz, s, d), F32),
        scratch_shapes=[pltpu.VMEM(((HALO + ts) * SUBLANES, LANES), F32), pltpu.VMEM((ts * SUBLANES, LANES), F32)],
        compiler_params=_params(2),
        name="conformer",
    )(x, shift, scale, gate, *consts)


def _ssm_in_kernel(x_ref, sh_ref, sc_ref, ng_ref, wz_ref, wx_ref, wdt_ref, cw_ref, cb_ref, dtb_ref,
                   z_ref, xs_ref, b_ref, c_ref, dt_ref, buf, cbuf):
    ts = x_ref.shape[1]
    d_inner = xs_ref.shape[2]
    gn = b_ref.shape[2]
    n_col = buf.shape[0]
    nh = d_inner // SSM_HEAD_DIM
    h = _rms_modulate(x_ref[0], ng_ref[...], sh_ref[0], sc_ref[0]).astype(BF16)
    z_ref[0] = jnp.dot(h, wz_ref[...], preferred_element_type=F32).astype(BF16)
    dt = jnp.dot(h, wdt_ref[...], preferred_element_type=F32) + dtb_ref[...]
    dt = jnp.maximum(dt, 0.0) + jnp.log(1.0 + jnp.exp(-jnp.abs(dt)))
    dt_ref[0] = jnp.where(lax.broadcasted_iota(I32, dt.shape, 1) < nh, dt, 0.0)

    @pl.when(pl.program_id(1) == 0)
    def _():
        buf[:, 0:SSM_HALO * SUBLANES, :] = jnp.zeros((n_col, SSM_HALO * SUBLANES, LANES), F32)

    outs = []
    for q in range(d_inner // TILE_CH):
        outs.append((xs_ref, q * TILE_CH))
    for q in range(gn // TILE_CH):
        outs.append((b_ref, q * TILE_CH))
    for q in range(gn // TILE_CH):
        outs.append((c_ref, q * TILE_CH))
    for q in range(n_col):
        xq = jnp.dot(h, wx_ref[:, q * TILE_CH:(q + 1) * TILE_CH], preferred_element_type=F32)
        _store_token_major(buf.at[q], SSM_HALO, xq)
        _token_major_conv(buf.at[q], cbuf.at[q], cw_ref.at[q], cb_ref[q], ts, SSM_CONV, SSM_HALO, _silu,
                          unrolled=True)
        buf[q, 0:SSM_HALO * SUBLANES, :] = buf[q, ts * SUBLANES:(ts + SSM_HALO) * SUBLANES, :]
        ref, lo = outs[q]
        ref[0, :, lo:lo + TILE_CH] = _load_token_major(cbuf.at[q], 0, ts).astype(BF16)


def _ssm_in(x, shift, scale, norm_g, in_w, conv_w, conv_b, dt_bias):
    bsz, s, d = x.shape
    ts = SEQ_TILE
    nh = dt_bias.shape[0]
    d_inner = nh * SSM_HEAD_DIM
    gn = SSM_GROUPS * SSM_STATE
    conv_dim = d_inner + 2 * gn
    n_col = conv_dim // TILE_CH
    assert d_inner % TILE_CH == 0 and gn % TILE_CH == 0
    w = in_w.astype(BF16)
    lane_pad = lambda a: jnp.pad(a, ((0, 0), (0, LANES - nh)))
    cw = conv_w.reshape(SSM_CONV, n_col, SUBLANES, LANES).transpose(1, 0, 2, 3).reshape(n_col, SSM_CONV * SUBLANES, LANES)
    cb = conv_b.reshape(n_col, SUBLANES, LANES)
    consts = [norm_g.reshape(1, d), w[:, :d_inner], w[:, d_inner:d_inner + conv_dim],
              lane_pad(w[:, d_inner + conv_dim:]), cw, cb, lane_pad(dt_bias.reshape(1, nh))]
    per_batch = pl.BlockSpec((1, 1, d), lambda b, j: (b, 0, 0))
    full = lambda a: pl.BlockSpec(a.shape, lambda b, j: (0,) * a.ndim)
    tile = lambda n: pl.BlockSpec((1, ts, n), lambda b, j: (b, j, 0))
    return pl.pallas_call(
        _ssm_in_kernel,
        grid=(bsz, s // ts),
        in_specs=[tile(d), per_batch, per_batch] + [full(a) for a in consts],
        out_specs=[tile(d_inner), tile(d_inner), tile(gn), tile(gn), tile(LANES)],
        out_shape=[jax.ShapeDtypeStruct((bsz, s, d_inner), BF16), jax.ShapeDtypeStruct((bsz, s, d_inner), BF16),
                   jax.ShapeDtypeStruct((bsz, s, gn), BF16), jax.ShapeDtypeStruct((bsz, s, gn), BF16),
                   jax.ShapeDtypeStruct((bsz, s, LANES), F32)],
        scratch_shapes=[pltpu.VMEM((n_col, (SSM_HALO + ts) * SUBLANES, LANES), F32),
                        pltpu.VMEM((n_col, ts * SUBLANES, LANES), F32)],
        compiler_params=_params(2),
        name="ssm_in",
    )(x, shift, scale, *consts)


def _expand_heads(v, rep):
    hi, lo = _split_bf16(v)
    return (jnp.dot(hi, rep, preferred_element_type=F32) + jnp.dot(lo, rep, preferred_element_type=F32))


def _ssd_kernel(xs_ref, b_ref, c_ref, dt_ref, z_ref, alog_ref, dskip_ref, ng_ref, o_ref, state):
    L = xs_ref.shape[1]
    d_inner = xs_ref.shape[2]
    hp = dt_ref.shape[2]
    hg = d_inner // SSM_HEAD_DIM // SSM_GROUPS
    gw = hg * SSM_HEAD_DIM
    n = SSM_STATE

    @pl.when(pl.program_id(1) == 0)
    def _():
        state[...] = jnp.zeros(state.shape, F32)

    dt = dt_ref[0]
    a = dt * (-jnp.exp(alog_ref[...]))
    row = lax.broadcasted_iota(I32, (L, L), 0)
    col = lax.broadcasted_iota(I32, (L, L), 1)
    causal = row >= col
    tri = causal.astype(F32)
    cum = jnp.dot(tri, a, precision=HIGHEST, preferred_element_type=F32)
    cum_t = cum.T
    dt_t = dt.T
    total = cum[L - 1:L, :]

    head_of = lax.broadcasted_iota(I32, (hp, d_inner), 1) // SSM_HEAD_DIM
    rep = (head_of == lax.broadcasted_iota(I32, (hp, d_inner), 0)).astype(BF16)
    grow = _expand_heads(jnp.exp(cum), rep)
    to_end = _expand_heads(jnp.exp(total - cum) * dt, rep)
    keep = _expand_heads(jnp.exp(total), rep)
    head_in_group = lax.broadcasted_iota(I32, (L, gw), 1) // SSM_HEAD_DIM

    xs = xs_ref[0]
    xf = xs.astype(F32)
    for g in range(SSM_GROUPS):
        bg = b_ref[0, :, g * n:(g + 1) * n]
        cg = c_ref[0, :, g * n:(g + 1) * n]
        xg = xs[:, g * gw:(g + 1) * gw]
        cb = lax.dot_general(cg, bg, (((1,), (1,)), ((), ())), preferred_element_type=F32)
        st = state[g]
        y_off = jnp.dot(cg, st.astype(BF16), preferred_element_type=F32)
        y_diag = jnp.zeros((L, gw), F32)
        for hh in range(hg):
            h = g * hg + hh
            seg = cum[:, h:h + 1] - cum_t[h:h + 1, :]
            w = jnp.where(causal, jnp.exp(seg), 0.0) * cb * dt_t[h:h + 1, :]
            x_head = jnp.where(head_in_group == hh, xg, jnp.zeros_like(xg))
            y_diag = y_diag + jnp.dot(w.astype(BF16), x_head, preferred_element_type=F32)
        sl = slice(g * gw, (g + 1) * gw)
        y = y_diag + y_off * grow[:, sl] + dskip_ref[:, sl] * xf[:, sl]
        xw = (xf[:, sl] * to_end[:, sl]).astype(BF16)
        upd = lax.dot_general(bg, xw, (((0,), (0,)), ((), ())), preferred_element_type=F32)
        state[g] = st * keep[:, sl] + upd
        y = y * _silu(z_ref[0, :, sl].astype(F32))
        y = y * lax.rsqrt(jnp.mean(y * y, axis=-1, keepdims=True) + EPS)
        o_ref[0, :, sl] = (y * ng_ref[:, sl]).astype(BF16)


def _ssd(xs, bm, cm, dt, z, a_log, d_skip, norm_g):
    bsz, s, d_inner = xs.shape
    nh = a_log.shape[0]
    hp = dt.shape[2]
    gn = bm.shape[2]
    L = SSD_CHUNK
    gw = d_inner // SSM_GROUPS
    consts = [jnp.pad(a_log.reshape(1, nh), ((0, 0), (0, hp - nh))),
              jnp.repeat(d_skip, SSM_HEAD_DIM).reshape(1, d_inner), norm_g.reshape(1, d_inner)]
    tile = lambda n: pl.BlockSpec((1, L, n), lambda b, j: (b, j, 0))
    full = lambda a: pl.BlockSpec(a.shape, lambda b, j: (0,) * a.ndim)
    return pl.pallas_call(
        _ssd_kernel,
        grid=(bsz, s // L),
        in_specs=[tile(d_inner), tile(gn), tile(gn), tile(hp), tile(d_inner)] + [full(a) for a in consts],
        out_specs=tile(d_inner),
        out_shape=jax.ShapeDtypeStruct((bsz, s, d_inner), BF16),
        scratch_shapes=[pltpu.VMEM((SSM_GROUPS, SSM_STATE, gw), F32)],
        compiler_params=_params(2),
        name="ssd",
    )(xs, bm, cm, dt, z, *consts)


def _out_proj_kernel(y_ref, x_ref, gate_ref, w_ref, o_ref):
    o_ref[0] = x_ref[0] + gate_ref[0] * jnp.dot(y_ref[0], w_ref[...], preferred_element_type=F32)


def _out_proj(y, x, gate, out_w):
    bsz, s, d = x.shape
    k = y.shape[2]
    ts = 512
    return pl.pallas_call(
        _out_proj_kernel,
        grid=(bsz, s // ts),
        in_specs=[pl.BlockSpec((1, ts, k), lambda b, j: (b, j, 0)), pl.BlockSpec((1, ts, d), lambda b, j: (b, j, 0)),
                  pl.BlockSpec((1, 1, d), lambda b, j: (b, 0, 0)), pl.BlockSpec((k, d), lambda b, j: (0, 0))],
        out_specs=pl.BlockSpec((1, ts, d), lambda b, j: (b, j, 0)),
        out_shape=jax.ShapeDtypeStruct((bsz, s, d), F32),
        compiler_params=_params(2),
        name="out_proj",
    )(y, x, gate, out_w.astype(BF16))


def _route_kernel(x_ref, sh_ref, sc_ref, ng_ref, wt_ref, rb_ref, h_ref, gate_ref, lpos_ref, info_ref, cnt_ref, running):
    tm = x_ref.shape[0]
    ne = wt_ref.shape[0]

    @pl.when(pl.program_id(0) == 0)
    def _():
        running[...] = jnp.zeros(running.shape, F32)

    h = _rms_modulate(x_ref[...], ng_ref[...], sh_ref[0], sc_ref[0])
    _store_token_major(h_ref, 0, h)
    h_hi, h_lo = _split_bf16(h)
    w_hi, w_lo = _split_bf16(wt_ref[...])
    nt = (((1,), (1,)), ((), ()))
    logits = (lax.dot_general(w_hi, h_hi, nt, preferred_element_type=F32)
              + lax.dot_general(w_hi, h_lo, nt, preferred_element_type=F32)
              + lax.dot_general(w_lo, h_hi, nt, preferred_element_type=F32)) + rb_ref[...]

    e_iota = lax.broadcasted_iota(I32, (ne, tm), 0).astype(F32)
    vals = logits
    tops, sels = [], []
    for _ in range(TOP_K):
        m = jnp.max(vals, axis=0, keepdims=True)
        idx = jnp.min(jnp.where(vals == m, e_iota, float(ne)), axis=0, keepdims=True)
        sel = e_iota == idx
        vals = jnp.where(sel, -jnp.inf, vals)
        tops.append(m)
        sels.append(sel)
    exps = [jnp.exp(t - tops[0]) for t in tops]
    inv = 1.0 / (exps[0] + exps[1] + exps[2] + exps[3])

    member = jnp.zeros((ne, tm), F32)
    for sel in sels:
        member = member + sel.astype(F32)
    before = (lax.broadcasted_iota(I32, (tm, tm), 0) < lax.broadcasted_iota(I32, (tm, tm), 1)).astype(BF16)
    earlier = jnp.dot(member.astype(BF16), before, preferred_element_type=F32)
    cnt = jnp.sum(member, axis=1, keepdims=True)
    lower = (lax.broadcasted_iota(I32, (ne, ne), 0) > lax.broadcasted_iota(I32, (ne, ne), 1)).astype(BF16)
    off = jnp.dot(lower, jnp.broadcast_to(cnt, (ne, LANES)).astype(BF16), preferred_element_type=F32)[:, 0:1]
    for k in range(TOP_K):
        gate_ref[k:k + 1, :] = exps[k] * inv
        lpos_ref[k:k + 1, :] = jnp.sum(jnp.where(sels[k], earlier + off, 0.0), axis=0, keepdims=True).astype(I32)
    lane = lax.broadcasted_iota(I32, (ne, LANES), 1)
    info = jnp.where(lane == 0, cnt, jnp.where(lane == 1, running[...], jnp.where(lane == 2, off, 0.0)))
    info_ref[...] = info.astype(I32)
    running[...] = running[...] + cnt
    cnt_ref[...] = jnp.broadcast_to(running[...], cnt_ref.shape).astype(I32)


def _route(x2, shift, scale, norm_g, router_w, router_b, tokens_per_batch):
    t, d = x2.shape
    assert d == TILE_CH
    ne = router_w.shape[1]
    tm = TOKEN_TILE
    tiles_per_batch = tokens_per_batch // tm
    per_batch = pl.BlockSpec((1, 1, d), lambda i: (i // tiles_per_batch, 0, 0))
    full = lambda shape: pl.BlockSpec(shape, lambda i: (0,) * len(shape))
    kt = pl.BlockSpec((TOP_K, tm), lambda i: (0, i))
    return pl.pallas_call(
        _route_kernel,
        grid=(t // tm,),
        in_specs=[pl.BlockSpec((tm, d), lambda i: (i, 0)), per_batch, per_batch,
                  full((1, d)), full((ne, d)), full((ne, 1))],
        out_specs=[pl.BlockSpec((tm * SUBLANES, LANES), lambda i: (i, 0)), kt, kt,
                   pl.BlockSpec((ne, LANES), lambda i: (i, 0)), full((ne, LANES))],
        out_shape=[jax.ShapeDtypeStruct((t * SUBLANES, LANES), F32), jax.ShapeDtypeStruct((TOP_K, t), F32),
                   jax.ShapeDtypeStruct((TOP_K, t), I32), jax.ShapeDtypeStruct((t // tm * ne, LANES), I32),
                   jax.ShapeDtypeStruct((ne, LANES), I32)],
        scratch_shapes=[pltpu.VMEM((ne, 1), F32)],
        compiler_params=_params(1),
        name="route",
    )(x2, shift, scale, norm_g.reshape(1, d), router_w.T, router_b.reshape(ne, 1))


RUN_PIECES = tuple(1 << b for b in range(TOKEN_TILE.bit_length() - 1, -1, -1))


def _run_copies(n, src_ref, src_tok, dst_ref, dst_tok, sem, fn):
    for b in RUN_PIECES:
        done = n & ~(2 * b - 1)

        @pl.when((n & b) != 0)
        def _():
            src = pl.ds(pl.multiple_of((src_tok + done) * SUBLANES, SUBLANES), b * SUBLANES)
            dst = pl.ds(pl.multiple_of((dst_tok + done) * SUBLANES, SUBLANES), b * SUBLANES)
            fn(pltpu.make_async_copy(src_ref.at[src], dst_ref.at[dst], sem))


def _tile_runs(step, n_experts, cnt_ref, fn):
    def body(e, carry):
        j = step * n_experts + e
        fn(j, cnt_ref[j])
        return carry

    lax.fori_loop(0, n_experts, body, 0)


def _dispatch_kernel(zslot_ref, tslot_ref, tcnt_ref, toff_ref, h_ref, lpos_ref, xb_ref, zeros, stage, sem, zsem):
    tm = h_ref.shape[0] // SUBLANES
    block_rows = SLOT_BLOCK * SUBLANES
    ne = N_EXPERTS
    i = pl.program_id(0)
    last = pl.num_programs(0) - 1

    @pl.when(pl.program_id(0) == 0)
    def _():
        zeros[...] = jnp.zeros(zeros.shape, F32)

        def zcopy(e):
            rows = pl.ds(pl.multiple_of(zslot_ref[e] * SUBLANES, block_rows), block_rows)
            return pltpu.make_async_copy(zeros, xb_ref.at[rows], zsem)

        def zstart(e, carry):
            @pl.when(zslot_ref[e] >= 0)
            def _():
                zcopy(e).start()
            return carry

        def zwait(e, carry):
            @pl.when(zslot_ref[e] >= 0)
            def _():
                zcopy(e).wait()
            return carry

        lax.fori_loop(0, zslot_ref.shape[0], zstart, 0)
        lax.fori_loop(0, zslot_ref.shape[0], zwait, 0)

    buf = stage.at[i % 2]

    def place(t, carry):
        tile = h_ref[_tile_rows(t), :]
        for k in range(TOP_K):
            buf[_tile_rows(lpos_ref[t * TOP_K + k]), :] = tile
        return carry

    lax.fori_loop(0, tm, place, 0, unroll=4)

    def runs(step, fn):
        src, s = stage.at[step % 2], sem.at[step % 2]
        _tile_runs(step, ne, tcnt_ref,
                   lambda j, n: _run_copies(n, src, toff_ref[j], xb_ref, tslot_ref[j], s, fn))

    runs(i, lambda cp: cp.start())

    @pl.when(i > 0)
    def _():
        runs(i - 1, lambda cp: cp.wait())

    @pl.when(i == last)
    def _():
        runs(i, lambda cp: cp.wait())


def _dispatch(zslot, tslot, tcnt, toff, h, lpos_flat, n_slots):
    t = h.shape[0] // SUBLANES
    tm = TOKEN_TILE
    return pl.pallas_call(
        _dispatch_kernel,
        grid_spec=pltpu.PrefetchScalarGridSpec(
            num_scalar_prefetch=4,
            grid=(t // tm,),
            in_specs=[pl.BlockSpec((tm * SUBLANES, LANES), lambda i, *_: (i, 0)),
                      pl.BlockSpec((tm * TOP_K,), lambda i, *_: (i,), memory_space=pltpu.SMEM)],
            out_specs=pl.BlockSpec(memory_space=pl.ANY),
            scratch_shapes=[pltpu.VMEM((SLOT_BLOCK * SUBLANES, LANES), F32),
                            pltpu.VMEM((2, tm * TOP_K * SUBLANES, LANES), F32),
                            pltpu.SemaphoreType.DMA((2,)), pltpu.SemaphoreType.DMA(())],
        ),
        out_shape=jax.ShapeDtypeStruct((n_slots * SUBLANES, LANES), F32),
        compiler_params=_params(1),
        name="dispatch",
    )(zslot, tslot, tcnt, toff, h, lpos_flat)


def _experts_kernel(be_ref, nu_ref, x_ref, w1_ref, b1_ref, w2_ref, b2_ref, y_ref, w1b, w2b):
    i = pl.program_id(0)
    f = w2b.shape[0]
    used = i < nu_ref[0]
    changed = (i == 0) | (be_ref[i] != be_ref[jnp.maximum(i - 1, 0)])

    @pl.when(used & changed)
    def _():
        w1b[...] = w1_ref[0, 0].astype(BF16)
        w2b[...] = w2_ref[0, 0].astype(BF16)

    @pl.when(jnp.logical_not(used))
    def _():
        y_ref[...] = jnp.zeros(y_ref.shape, F32)

    @pl.when(used)
    def _():
        x = _load_token_major(x_ref, 0, SLOT_BLOCK).astype(BF16)
        gu = jnp.dot(x, w1b[...], preferred_element_type=F32) + b1_ref[0, 0]
        g = jnp.minimum(gu[:, :f], SWIGLU_LIMIT)
        u = jnp.clip(gu[:, f:], -SWIGLU_LIMIT, SWIGLU_LIMIT)
        act = (u + 1.0) * (g * _sigmoid(SWIGLU_ALPHA * g))
        y = jnp.dot(act.astype(BF16), w2b[...], preferred_element_type=F32) + b2_ref[0, 0]
        _store_token_major(y_ref, 0, y)


def _experts(layer, block_e, n_used, xb, w1, b1, w2, b2):
    n_slots = xb.shape[0] // SUBLANES
    _, ne, d, f2 = w1.shape
    f = w2.shape[2]
    assert d == TILE_CH
    nb = n_slots // SLOT_BLOCK
    blk = lambda i, be, nu: jnp.minimum(i, nu[0] - 1)
    rows = (SLOT_BLOCK * SUBLANES, LANES)
    per_e = lambda shape: pl.BlockSpec((1, 1) + shape, lambda i, be, nu: (layer, be[blk(i, be, nu)], 0, 0))
    return pl.pallas_call(
        _experts_kernel,
        grid_spec=pltpu.PrefetchScalarGridSpec(
            num_scalar_prefetch=2,
            grid=(nb,),
            in_specs=[pl.BlockSpec(rows, lambda i, be, nu: (blk(i, be, nu), 0)),
                      per_e((d, f2)), per_e((1, f2)), per_e((f, d)), per_e((1, d))],
            out_specs=pl.BlockSpec(rows, lambda i, be, nu: (i, 0)),
            scratch_shapes=[pltpu.VMEM((d, f2), BF16), pltpu.VMEM((f, d), BF16)],
        ),
        out_shape=jax.ShapeDtypeStruct((n_slots * SUBLANES, LANES), F32),
        compiler_params=_params(1),
        name="experts",
    )(block_e, n_used, xb, w1, b1.reshape(b1.shape[0], ne, 1, f2), w2, b2.reshape(b2.shape[0], ne, 1, d))


def _combine_kernel(final_norm, tslot_ref, tcnt_ref, toff_ref, x_ref, lpos_ref, gate_ref, g2_ref, fg_ref, ys_ref,
                    o_ref, stage, obuf, sem):
    tm = x_ref.shape[0]
    ne = N_EXPERTS
    i = pl.program_id(0)
    last = pl.num_programs(0) - 1

    def runs(step, fn):
        dst, s = stage.at[step % 2], sem.at[step % 2]
        _tile_runs(step, ne, tcnt_ref,
                   lambda j, n: _run_copies(n, ys_ref, tslot_ref[j], dst, toff_ref[j], s, fn))

    @pl.when(i == 0)
    def _():
        runs(i, lambda cp: cp.start())

    @pl.when(i < last)
    def _():
        runs(i + 1, lambda cp: cp.start())

    runs(i, lambda cp: cp.wait())
    buf = stage.at[i % 2]

    def mix(t, carry):
        acc = gate_ref[t * TOP_K] * buf[_tile_rows(lpos_ref[t * TOP_K]), :]
        for k in range(1, TOP_K):
            acc = acc + gate_ref[t * TOP_K + k] * buf[_tile_rows(lpos_ref[t * TOP_K + k]), :]
        obuf[_tile_rows(t), :] = acc
        return carry

    lax.fori_loop(0, tm, mix, 0, unroll=4)
    x = x_ref[...] + g2_ref[0] * _load_token_major(obuf, 0, tm)
    if final_norm:
        x = x * lax.rsqrt(jnp.mean(x * x, axis=-1, keepdims=True) + EPS) * fg_ref[...]
    o_ref[...] = x


def _combine(tslot, tcnt, toff, x2, lpos_flat, gate_flat, g2, final_g, ys, tokens_per_batch, final_norm):
    t, d = x2.shape
    tm = TOKEN_TILE
    tiles_per_batch = tokens_per_batch // tm
    pairs = pl.BlockSpec((tm * TOP_K,), lambda i, *_: (i,), memory_space=pltpu.SMEM)
    return pl.pallas_call(
        functools.partial(_combine_kernel, final_norm),
        grid_spec=pltpu.PrefetchScalarGridSpec(
            num_scalar_prefetch=3,
            grid=(t // tm,),
            in_specs=[pl.BlockSpec((tm, d), lambda i, *_: (i, 0)), pairs, pairs,
                      pl.BlockSpec((1, 1, d), lambda i, *_: (i // tiles_per_batch, 0, 0)),
                      pl.BlockSpec((1, d), lambda i, *_: (0, 0)),
                      pl.BlockSpec(memory_space=pl.ANY)],
            out_specs=pl.BlockSpec((tm, d), lambda i, *_: (i, 0)),
            scratch_shapes=[pltpu.VMEM((2, tm * TOP_K * SUBLANES, LANES), F32), pltpu.VMEM((tm * SUBLANES, LANES), F32),
                            pltpu.SemaphoreType.DMA((2,))],
        ),
        out_shape=jax.ShapeDtypeStruct((t, d), F32),
        compiler_params=_params(1),
        name="combine",
    )(tslot, tcnt, toff, x2, lpos_flat, gate_flat, g2, final_g.reshape(1, d), ys)


def _moe(layer, x, shift, scale, gate2, norm_g, router_w, router_b, w1, b1, w2, b2, final_g, final_norm):
    bsz, s, d = x.shape
    t = bsz * s
    ne = router_w.shape[1]
    x2 = x.reshape(t, d)
    h, gate, lpos, info, counts = _route(x2, shift, scale, norm_g, router_w, router_b, s)
    counts = counts[:, 0]
    padded = ((counts + SLOT_BLOCK - 1) // SLOT_BLOCK) * SLOT_BLOCK
    ends = jnp.cumsum(padded)
    pstart = (ends - padded).astype(I32)
    n_slots = t * TOP_K + ne * SLOT_BLOCK
    nb = n_slots // SLOT_BLOCK
    block_start = jnp.arange(nb, dtype=I32) * SLOT_BLOCK
    block_e = jnp.minimum(jnp.sum(block_start[:, None] >= ends[None, :], axis=1), ne - 1).astype(I32)
    n_used = (ends[-1:] // SLOT_BLOCK).astype(I32)
    trailing = n_used[0] + jnp.arange(nb - t * TOP_K // SLOT_BLOCK, dtype=I32)
    zslot = jnp.concatenate([jnp.where(padded > 0, ends - SLOT_BLOCK, -1),
                             jnp.where(trailing < nb, trailing * SLOT_BLOCK, -1)]).astype(I32)
    tcnt, tbase, toff = info[:, 0], info[:, 1], info[:, 2]
    tslot = tbase + jnp.tile(pstart, t // TOKEN_TILE)
    pair_major = lambda a: a.T.reshape(t * TOP_K)
    lpos_flat = pair_major(lpos)
    xb = _dispatch(zslot, tslot, tcnt, toff, h, lpos_flat, n_slots)
    ys = _experts(layer, block_e, n_used, xb, w1, b1, w2, b2)
    out = _combine(tslot, tcnt, toff, x2, lpos_flat, pair_major(gate), gate2, final_g, ys, s, final_norm)
    return out.reshape(bsz, s, d)


def kernel(x, c, ada_w, ada_b, mix_norm_g, ffn_norm_g, conf_pw1_w, conf_pw1_b, conf_dw_w, conf_dw_b, conf_ln_g, conf_ln_b, conf_pw2_w, conf_pw2_b, ssm_in_w, ssm_conv_w, ssm_conv_b, ssm_dt_bias, ssm_a_log, ssm_d, ssm_norm_g, ssm_out_w, router_w, router_b, exp_w1, exp_b1, exp_w2, exp_b2, final_norm_g):
    depth = ada_w.shape[0]
    bsz, s, d = x.shape
    mod = _ada(c, ada_w, ada_b).reshape(depth, bsz, 6, 1, d)
    for i in range(depth):
        sh1, sc1, g1, sh2, sc2, g2 = (mod[i, :, m] for m in range(6))
        j = i // 2
        if i % 2 == 0:
            x = _conformer(x, sh1, sc1, g1, mix_norm_g[i], conf_pw1_w[j], conf_pw1_b[j], conf_dw_w[j], conf_dw_b[j],
                           conf_ln_g[j], conf_ln_b[j], conf_pw2_w[j], conf_pw2_b[j])
        else:
            z, xs, bm, cm, dt = _ssm_in(x, sh1, sc1, mix_norm_g[i], ssm_in_w[j], ssm_conv_w[j], ssm_conv_b[j],
                                        ssm_dt_bias[j])
            y = _ssd(xs, bm, cm, dt, z, ssm_a_log[j], ssm_d[j], ssm_norm_g[j])
            x = _out_proj(y, x, g1, ssm_out_w[j])
        x = _moe(i, x, sh2, sc2, g2, ffn_norm_g[i], router_w[i], router_b[i], exp_w1, exp_b1, exp_w2, exp_b2,
                 final_norm_g, final_norm=(i == depth - 1))
    return x
```

```python
import functools

import jax
import jax.numpy as jnp
from jax import lax
from jax.experimental import pallas as pl
from jax.experimental.pallas import tpu as pltpu

F32 = jnp.float32
BF16 = jnp.bfloat16
I32 = jnp.int32

EPS = 1e-5
CONV_WIDTH = 31
SSM_HEAD_DIM = 64
SSM_GROUPS = 8
SSM_STATE = 128
SSM_CONV = 4
SSD_CHUNK = 128
N_EXPERTS = 32
TOP_K = 4
SWIGLU_LIMIT = 7.0
SWIGLU_ALPHA = 1.702

LANES = 128
SUBLANES = 8
TILE_CH = LANES * SUBLANES
SLOT_BLOCK = 512
TOKEN_TILE = 256
SEQ_TILE = 256
HALO = 32
SSM_HALO = 8
CONV_GROUP = 8
VMEM_LIMIT = 56 * 1024 * 1024

HIGHEST = lax.Precision.HIGHEST


def _params(n_axes, vmem=VMEM_LIMIT):
    return pltpu.CompilerParams(dimension_semantics=("arbitrary",) * n_axes, vmem_limit_bytes=vmem)


def _sigmoid(v):
    return 1.0 / (1.0 + jnp.exp(-v))


def _silu(v):
    return v * _sigmoid(v)


def _rms_modulate(x, g, shift, scale):
    ms = jnp.mean(x * x, axis=-1, keepdims=True)
    return (x * lax.rsqrt(ms + EPS) * g) * (1.0 + scale) + shift


def _split_bf16(v):
    hi = v.astype(BF16)
    lo = (v - hi.astype(F32)).astype(BF16)
    return hi, lo


def _store_token_major(ref, tok0, v):
    n, d = v.shape
    assert d == TILE_CH and n % SUBLANES == 0
    for th in range(n // SUBLANES):
        for j in range(SUBLANES):
            ref[pl.ds((tok0 + th * SUBLANES) * SUBLANES + j, SUBLANES, stride=SUBLANES), :] = (
                v[th * SUBLANES:(th + 1) * SUBLANES, j * LANES:(j + 1) * LANES])


def _load_token_major(ref, tok0, n):
    rows = []
    for th in range(n // SUBLANES):
        rows.append(jnp.concatenate(
            [ref[pl.ds((tok0 + th * SUBLANES) * SUBLANES + j, SUBLANES, stride=SUBLANES), :] for j in range(SUBLANES)],
            axis=1))
    return jnp.concatenate(rows, axis=0)


def _tile_rows(tok):
    if isinstance(tok, int):
        return pl.ds(tok * SUBLANES, SUBLANES)
    return pl.ds(pl.multiple_of(tok * SUBLANES, SUBLANES), SUBLANES)


def _token_major_conv(src, dst, w_ref, b_tile, n_tokens, width, halo, post, unrolled=False):
    base = halo - (width - 1)

    def group(gi, carry):
        t0 = gi * CONV_GROUP
        tiles = [src[_tile_rows(t0 + base + i), :] for i in range(width + CONV_GROUP - 1)]
        for o in range(CONV_GROUP):
            acc = [b_tile, None]
            for k in range(width):
                term = w_ref[k * SUBLANES:(k + 1) * SUBLANES, :] * tiles[o + k]
                acc[k % 2] = term if acc[k % 2] is None else acc[k % 2] + term
            dst[_tile_rows(t0 + o), :] = post(acc[0] + acc[1])
        return carry

    if unrolled:
        for gi in range(n_tokens // CONV_GROUP):
            group(gi, 0)
    else:
        lax.fori_loop(0, n_tokens // CONV_GROUP, group, 0)


def _ada_kernel(c_ref, w_ref, b_ref, o_ref):
    c = c_ref[...]
    o_ref[0] = jnp.dot(_silu(c), w_ref[0], precision=HIGHEST, preferred_element_type=F32) + b_ref[0]


def _ada(c, ada_w, ada_b):
    depth, d, n = ada_w.shape
    bsz = c.shape[0]
    tn = 1536
    return pl.pallas_call(
        _ada_kernel,
        grid=(depth, n // tn),
        in_specs=[pl.BlockSpec((bsz, d), lambda l, j: (0, 0)),
                  pl.BlockSpec((1, d, tn), lambda l, j: (l, 0, j)),
                  pl.BlockSpec((1, 1, tn), lambda l, j: (l, 0, j))],
        out_specs=pl.BlockSpec((1, bsz, tn), lambda l, j: (l, 0, j)),
        out_shape=jax.ShapeDtypeStruct((depth, bsz, n), F32),
        compiler_params=_params(2),
        name="ada",
    )(c, ada_w, ada_b.reshape(depth, 1, n))


def _conformer_kernel(x_ref, sh_ref, sc_ref, gate_ref, ng_ref, w1_ref, b1_ref, dw_ref, dwb_ref,
                      lng_ref, lnb_ref, w2_ref, b2_ref, o_ref, ubuf, cbuf):
    ts, d = x_ref.shape[1], x_ref.shape[2]
    x = x_ref[0]
    h = _rms_modulate(x, ng_ref[...], sh_ref[0], sc_ref[0])
    u = jnp.dot(h.astype(BF16), w1_ref[...], preferred_element_type=F32) + b1_ref[...]
    u = u[:, :d] * _sigmoid(u[:, d:])

    @pl.when(pl.program_id(1) == 0)
    def _():
        ubuf[0:HALO * SUBLANES, :] = jnp.zeros((HALO * SUBLANES, LANES), F32)

    _store_token_major(ubuf, HALO, u)
    _token_major_conv(ubuf, cbuf, dw_ref, dwb_ref[...], ts, CONV_WIDTH, HALO, lambda a: a, unrolled=True)
    ubuf[0:HALO * SUBLANES, :] = ubuf[ts * SUBLANES:(ts + HALO) * SUBLANES, :]

    v = _load_token_major(cbuf, 0, ts)
    mu = jnp.mean(v, axis=-1, keepdims=True)
    vc = v - mu
    var = jnp.mean(vc * vc, axis=-1, keepdims=True)
    v = _silu(vc * lax.rsqrt(var + EPS) * lng_ref[...] + lnb_ref[...])
    y = jnp.dot(v.astype(BF16), w2_ref[...], preferred_element_type=F32) + b2_ref[...]
    o_ref[0] = x + gate_ref[0] * y


def _conformer(x, shift, scale, gate, norm_g, pw1_w, pw1_b, dw_w, dw_b, ln_g, ln_b, pw2_w, pw2_b):
    bsz, s, d = x.shape
    assert d == TILE_CH
    ts = SEQ_TILE
    row = lambda a: a.reshape(1, -1)
    per_batch = pl.BlockSpec((1, 1, d), lambda b, j: (b, 0, 0))
    full = lambda a: pl.BlockSpec(a.shape, lambda b, j: (0,) * a.ndim)
    consts = [row(norm_g), pw1_w.astype(BF16), row(pw1_b), dw_w.reshape(CONV_WIDTH * SUBLANES, LANES),
              dw_b.reshape(SUBLANES, LANES), row(ln_g), row(ln_b), pw2_w.astype(BF16), row(pw2_b)]
    return pl.pallas_call(
        _conformer_kernel,
        grid=(bsz, s // ts),
        in_specs=[pl.BlockSpec((1, ts, d), lambda b, j: (b, j, 0)), per_batch, per_batch, per_batch]
                 + [full(a) for a in consts],
        out_specs=pl.BlockSpec((1, ts, d), lambda b, j: (b, j, 0)),
        out_shape=jax.ShapeDtypeStruct((bsz, s, d), F32),
        scratch_shapes=[pltpu.VMEM(((HALO + ts) * SUBLANES, LANES), F32), pltpu.VMEM((ts * SUBLANES, LANES), F32)],
        compiler_params=_params(2),
        name="conformer",
    )(x, shift, scale, gate, *consts)


def _ssm_in_kernel(x_ref, sh_ref, sc_ref, ng_ref, wz_ref, wx_ref, wdt_ref, cw_ref, cb_ref, dtb_ref,
                   z_ref, xs_ref, b_ref, c_ref, dt_ref, buf, cbuf):
    ts = x_ref.shape[1]
    d_inner = xs_ref.shape[2]
    gn = b_ref.shape[2]
    n_col = buf.shape[0]
    nh = d_inner // SSM_HEAD_DIM
    h = _rms_modulate(x_ref[0], ng_ref[...], sh_ref[0], sc_ref[0]).astype(BF16)
    z_ref[0] = jnp.dot(h, wz_ref[...], preferred_element_type=F32).astype(BF16)
    dt = jnp.dot(h, wdt_ref[...], preferred_element_type=F32) + dtb_ref[...]
    dt = jnp.maximum(dt, 0.0) + jnp.log(1.0 + jnp.exp(-jnp.abs(dt)))
    dt_ref[0] = jnp.where(lax.broadcasted_iota(I32, dt.shape, 1) < nh, dt, 0.0)

    @pl.when(pl.program_id(1) == 0)
    def _():
        buf[:, 0:SSM_HALO * SUBLANES, :] = jnp.zeros((n_col, SSM_HALO * SUBLANES, LANES), F32)

    outs = []
    for q in range(d_inner // TILE_CH):
        outs.append((xs_ref, q * TILE_CH))
    for q in range(gn // TILE_CH):
        outs.append((b_ref, q * TILE_CH))
    for q in range(gn // TILE_CH):
        outs.append((c_ref, q * TILE_CH))
    for q in range(n_col):
        xq = jnp.dot(h, wx_ref[:, q * TILE_CH:(q + 1) * TILE_CH], preferred_element_type=F32)
        _store_token_major(buf.at[q], SSM_HALO, xq)
        _token_major_conv(buf.at[q], cbuf.at[q], cw_ref.at[q], cb_ref[q], ts, SSM_CONV, SSM_HALO, _silu,
                          unrolled=True)
        buf[q, 0:SSM_HALO * SUBLANES, :] = buf[q, ts * SUBLANES:(ts + SSM_HALO) * SUBLANES, :]
        ref, lo = outs[q]
        ref[0, :, lo:lo + TILE_CH] = _load_token_major(cbuf.at[q], 0, ts).astype(BF16)


def _ssm_in(x, shift, scale, norm_g, in_w, conv_w, conv_b, dt_bias):
    bsz, s, d = x.shape
    ts = SEQ_TILE
    nh = dt_bias.shape[0]
    d_inner = nh * SSM_HEAD_DIM
    gn = SSM_GROUPS * SSM_STATE
    conv_dim = d_inner + 2 * gn
    n_col = conv_dim // TILE_CH
    assert d_inner % TILE_CH == 0 and gn % TILE_CH == 0
    w = in_w.astype(BF16)
    lane_pad = lambda a: jnp.pad(a, ((0, 0), (0, LANES - nh)))
    cw = conv_w.reshape(SSM_CONV, n_col, SUBLANES, LANES).transpose(1, 0, 2, 3).reshape(n_col, SSM_CONV * SUBLANES, LANES)
    cb = conv_b.reshape(n_col, SUBLANES, LANES)
    consts = [norm_g.reshape(1, d), w[:, :d_inner], w[:, d_inner:d_inner + conv_dim],
              lane_pad(w[:, d_inner + conv_dim:]), cw, cb, lane_pad(dt_bias.reshape(1, nh))]
    per_batch = pl.BlockSpec((1, 1, d), lambda b, j: (b, 0, 0))
    full = lambda a: pl.BlockSpec(a.shape, lambda b, j: (0,) * a.ndim)
    tile = lambda n: pl.BlockSpec((1, ts, n), lambda b, j: (b, j, 0))
    return pl.pallas_call(
        _ssm_in_kernel,
        grid=(bsz, s // ts),
        in_specs=[tile(d), per_batch, per_batch] + [full(a) for a in consts],
        out_specs=[tile(d_inner), tile(d_inner), tile(gn), tile(gn), tile(LANES)],
        out_shape=[jax.ShapeDtypeStruct((bsz, s, d_inner), BF16), jax.ShapeDtypeStruct((bsz, s, d_inner), BF16),
                   jax.ShapeDtypeStruct((bsz, s, gn), BF16), jax.ShapeDtypeStruct((bsz, s, gn), BF16),
                   jax.ShapeDtypeStruct((bsz, s, LANES), F32)],
        scratch_shapes=[pltpu.VMEM((n_col, (SSM_HALO + ts) * SUBLANES, LANES), F32),
                        pltpu.VMEM((n_col, ts * SUBLANES, LANES), F32)],
        compiler_params=_params(2),
        name="ssm_in",
    )(x, shift, scale, *consts)


def _expand_heads(v, rep):
    hi, lo = _split_bf16(v)
    return (jnp.dot(hi, rep, preferred_element_type=F32) + jnp.dot(lo, rep, preferred_element_type=F32))


def _ssd_kernel(xs_ref, b_ref, c_ref, dt_ref, z_ref, alog_ref, dskip_ref, ng_ref, o_ref, state):
    L = xs_ref.shape[1]
    d_inner = xs_ref.shape[2]
    hp = dt_ref.shape[2]
    hg = d_inner // SSM_HEAD_DIM // SSM_GROUPS
    gw = hg * SSM_HEAD_DIM
    n = SSM_STATE

    @pl.when(pl.program_id(1) == 0)
    def _():
        state[...] = jnp.zeros(state.shape, F32)

    dt = dt_ref[0]
    a = dt * (-jnp.exp(alog_ref[...]))
    row = lax.broadcasted_iota(I32, (L, L), 0)
    col = lax.broadcasted_iota(I32, (L, L), 1)
    causal = row >= col
    tri = causal.astype(F32)
    cum = jnp.dot(tri, a, precision=HIGHEST, preferred_element_type=F32)
    cum_t = cum.T
    dt_t = dt.T
    total = cum[L - 1:L, :]

    head_of = lax.broadcasted_iota(I32, (hp, d_inner), 1) // SSM_HEAD_DIM
    rep = (head_of == lax.broadcasted_iota(I32, (hp, d_inner), 0)).astype(BF16)
    grow = _expand_heads(jnp.exp(cum), rep)
    to_end = _expand_heads(jnp.exp(total - cum) * dt, rep)
    keep = _expand_heads(jnp.exp(total), rep)
    head_in_group = lax.broadcasted_iota(I32, (L, gw), 1) // SSM_HEAD_DIM

    xs = xs_ref[0]
    xf = xs.astype(F32)
    for g in range(SSM_GROUPS):
        bg = b_ref[0, :, g * n:(g + 1) * n]
        cg = c_ref[0, :, g * n:(g + 1) * n]
        xg = xs[:, g * gw:(g + 1) * gw]
        cb = lax.dot_general(cg, bg, (((1,), (1,)), ((), ())), preferred_element_type=F32)
        st = state[g]
        y_off = jnp.dot(cg, st.astype(BF16), preferred_element_type=F32)
        y_diag = jnp.zeros((L, gw), F32)
        for hh in range(hg):
            h = g * hg + hh
            seg = cum[:, h:h + 1] - cum_t[h:h + 1, :]
            w = jnp.where(causal, jnp.exp(seg), 0.0) * cb * dt_t[h:h + 1, :]
            x_head = jnp.where(head_in_group == hh, xg, jnp.zeros_like(xg))
            y_diag = y_diag + jnp.dot(w.astype(BF16), x_head, preferred_element_type=F32)
        sl = slice(g * gw, (g + 1) * gw)
        y = y_diag + y_off * grow[:, sl] + dskip_ref[:, sl] * xf[:, sl]
        xw = (xf[:, sl] * to_end[:, sl]).astype(BF16)
        upd = lax.dot_general(bg, xw, (((0,), (0,)), ((), ())), preferred_element_type=F32)
        state[g] = st * keep[:, sl] + upd
        y = y * _silu(z_ref[0, :, sl].astype(F32))
        y = y * lax.rsqrt(jnp.mean(y * y, axis=-1, keepdims=True) + EPS)
        o_ref[0, :, sl] = (y * ng_ref[:, sl]).astype(BF16)


def _ssd(xs, bm, cm, dt, z, a_log, d_skip, norm_g):
    bsz, s, d_inner = xs.shape
    nh = a_log.shape[0]
    hp = dt.shape[2]
    gn = bm.shape[2]
    L = SSD_CHUNK
    gw = d_inner // SSM_GROUPS
    consts = [jnp.pad(a_log.reshape(1, nh), ((0, 0), (0, hp - nh))),
              jnp.repeat(d_skip, SSM_HEAD_DIM).reshape(1, d_inner), norm_g.reshape(1, d_inner)]
    tile = lambda n: pl.BlockSpec((1, L, n), lambda b, j: (b, j, 0))
    full = lambda a: pl.BlockSpec(a.shape, lambda b, j: (0,) * a.ndim)
    return pl.pallas_call(
        _ssd_kernel,
        grid=(bsz, s // L),
        in_specs=[tile(d_inner), tile(gn), tile(gn), tile(hp), tile(d_inner)] + [full(a) for a in consts],
        out_specs=tile(d_inner),
        out_shape=jax.ShapeDtypeStruct((bsz, s, d_inner), BF16),
        scratch_shapes=[pltpu.VMEM((SSM_GROUPS, SSM_STATE, gw), F32)],
        compiler_params=_params(2),
        name="ssd",
    )(xs, bm, cm, dt, z, *consts)


def _out_proj_kernel(y_ref, x_ref, gate_ref, w_ref, o_ref):
    o_ref[0] = x_ref[0] + gate_ref[0] * jnp.dot(y_ref[0], w_ref[...], preferred_element_type=F32)


def _out_proj(y, x, gate, out_w):
    bsz, s, d = x.shape
    k = y.shape[2]
    ts = 512
    return pl.pallas_call(
        _out_proj_kernel,
        grid=(bsz, s // ts),
        in_specs=[pl.BlockSpec((1, ts, k), lambda b, j: (b, j, 0)), pl.BlockSpec((1, ts, d), lambda b, j: (b, j, 0)),
                  pl.BlockSpec((1, 1, d), lambda b, j: (b, 0, 0)), pl.BlockSpec((k, d), lambda b, j: (0, 0))],
        out_specs=pl.BlockSpec((1, ts, d), lambda b, j: (b, j, 0)),
        out_shape=jax.ShapeDtypeStruct((bsz, s, d), F32),
        compiler_params=_params(2),
        name="out_proj",
    )(y, x, gate, out_w.astype(BF16))


def _route_kernel(x_ref, sh_ref, sc_ref, ng_ref, wt_ref, rb_ref, h_ref, gate_ref, lpos_ref, info_ref, cnt_ref, running):
    tm = x_ref.shape[0]
    ne = wt_ref.shape[0]

    @pl.when(pl.program_id(0) == 0)
    def _():
        running[...] = jnp.zeros(running.shape, F32)

    h = _rms_modulate(x_ref[...], ng_ref[...], sh_ref[0], sc_ref[0])
    _store_token_major(h_ref, 0, h)
    h_hi, h_lo = _split_bf16(h)
    w_hi, w_lo = _split_bf16(wt_ref[...])
    nt = (((1,), (1,)), ((), ()))
    logits = (lax.dot_general(w_hi, h_hi, nt, preferred_element_type=F32)
              + lax.dot_general(w_hi, h_lo, nt, preferred_element_type=F32)
              + lax.dot_general(w_lo, h_hi, nt, preferred_element_type=F32)) + rb_ref[...]

    e_iota = lax.broadcasted_iota(I32, (ne, tm), 0).astype(F32)
    vals = logits
    tops, sels = [], []
    for _ in range(TOP_K):
        m = jnp.max(vals, axis=0, keepdims=True)
        idx = jnp.min(jnp.where(vals == m, e_iota, float(ne)), axis=0, keepdims=True)
        sel = e_iota == idx
        vals = jnp.where(sel, -jnp.inf, vals)
        tops.append(m)
        sels.append(sel)
    exps = [jnp.exp(t - tops[0]) for t in tops]
    inv = 1.0 / (exps[0] + exps[1] + exps[2] + exps[3])

    member = jnp.zeros((ne, tm), F32)
    for sel in sels:
        member = member + sel.astype(F32)
    before = (lax.broadcasted_iota(I32, (tm, tm), 0) < lax.broadcasted_iota(I32, (tm, tm), 1)).astype(BF16)
    earlier = jnp.dot(member.astype(BF16), before, preferred_element_type=F32)
    cnt = jnp.sum(member, axis=1, keepdims=True)
    lower = (lax.broadcasted_iota(I32, (ne, ne), 0) > lax.broadcasted_iota(I32, (ne, ne), 1)).astype(BF16)
    off = jnp.dot(lower, jnp.broadcast_to(cnt, (ne, LANES)).astype(BF16), preferred_element_type=F32)[:, 0:1]
    for k in range(TOP_K):
        gate_ref[k:k + 1, :] = exps[k] * inv
        lpos_ref[k:k + 1, :] = jnp.sum(jnp.where(sels[k], earlier + off, 0.0), axis=0, keepdims=True).astype(I32)
    lane = lax.broadcasted_iota(I32, (ne, LANES), 1)
    info = jnp.where(lane == 0, cnt, jnp.where(lane == 1, running[...], jnp.where(lane == 2, off, 0.0)))
    info_ref[...] = info.astype(I32)
    running[...] = running[...] + cnt
    cnt_ref[...] = jnp.broadcast_to(running[...], cnt_ref.shape).astype(I32)


def _route(x2, shift, scale, norm_g, router_w, router_b, tokens_per_batch):
    t, d = x2.shape
    assert d == TILE_CH
    ne = router_w.shape[1]
    tm = TOKEN_TILE
    tiles_per_batch = tokens_per_batch // tm
    per_batch = pl.BlockSpec((1, 1, d), lambda i: (i // tiles_per_batch, 0, 0))
    full = lambda shape: pl.BlockSpec(shape, lambda i: (0,) * len(shape))
    kt = pl.BlockSpec((TOP_K, tm), lambda i: (0, i))
    return pl.pallas_call(
        _route_kernel,
        grid=(t // tm,),
        in_specs=[pl.BlockSpec((tm, d), lambda i: (i, 0)), per_batch, per_batch,
                  full((1, d)), full((ne, d)), full((ne, 1))],
        out_specs=[pl.BlockSpec((tm * SUBLANES, LANES), lambda i: (i, 0)), kt, kt,
                   pl.BlockSpec((ne, LANES), lambda i: (i, 0)), full((ne, LANES))],
        out_shape=[jax.ShapeDtypeStruct((t * SUBLANES, LANES), F32), jax.ShapeDtypeStruct((TOP_K, t), F32),
                   jax.ShapeDtypeStruct((TOP_K, t), I32), jax.ShapeDtypeStruct((t // tm * ne, LANES), I32),
                   jax.ShapeDtypeStruct((ne, LANES), I32)],
        scratch_shapes=[pltpu.VMEM((ne, 1), F32)],
        compiler_params=_params(1),
        name="route",
    )(x2, shift, scale, norm_g.reshape(1, d), router_w.T, router_b.reshape(ne, 1))


RUN_PIECES = tuple(1 << b for b in range(TOKEN_TILE.bit_length() - 1, -1, -1))


def _run_copies(n, src_ref, src_tok, dst_ref, dst_tok, sem, fn):
    for b in RUN_PIECES:
        done = n & ~(2 * b - 1)

        @pl.when((n & b) != 0)
        def _():
            src = pl.ds(pl.multiple_of((src_tok + done) * SUBLANES, SUBLANES), b * SUBLANES)
            dst = pl.ds(pl.multiple_of((dst_tok + done) * SUBLANES, SUBLANES), b * SUBLANES)
            fn(pltpu.make_async_copy(src_ref.at[src], dst_ref.at[dst], sem))


def _tile_runs(step, n_experts, cnt_ref, fn):
    def body(e, carry):
        j = step * n_experts + e
        fn(j, cnt_ref[j])
        return carry

    lax.fori_loop(0, n_experts, body, 0)


def _dispatch_kernel(zslot_ref, tslot_ref, tcnt_ref, toff_ref, h_ref, lpos_ref, xb_ref, zeros, stage, sem, zsem):
    tm = h_ref.shape[0] // SUBLANES
    block_rows = SLOT_BLOCK * SUBLANES
    ne = N_EXPERTS
    i = pl.program_id(0)
    last = pl.num_programs(0) - 1

    @pl.when(pl.program_id(0) == 0)
    def _():
        zeros[...] = jnp.zeros(zeros.shape, F32)

        def zcopy(e):
            rows = pl.ds(pl.multiple_of(zslot_ref[e] * SUBLANES, block_rows), block_rows)
            return pltpu.make_async_copy(zeros, xb_ref.at[rows], zsem)

        def zstart(e, carry):
            @pl.when(zslot_ref[e] >= 0)
            def _():
                zcopy(e).start()
            return carry

        def zwait(e, carry):
            @pl.when(zslot_ref[e] >= 0)
            def _():
                zcopy(e).wait()
            return carry

        lax.fori_loop(0, zslot_ref.shape[0], zstart, 0)
        lax.fori_loop(0, zslot_ref.shape[0], zwait, 0)

    buf = stage.at[i % 2]

    def place(t, carry):
        tile = h_ref[_tile_rows(t), :]
        for k in range(TOP_K):
            buf[_tile_rows(lpos_ref[t * TOP_K + k]), :] = tile
        return carry

    lax.fori_loop(0, tm, place, 0, unroll=4)

    def runs(step, fn):
        src, s = stage.at[step % 2], sem.at[step % 2]
        _tile_runs(step, ne, tcnt_ref,
                   lambda j, n: _run_copies(n, src, toff_ref[j], xb_ref, tslot_ref[j], s, fn))

    runs(i, lambda cp: cp.start())

    def wait_all(step):
        rows = stage.shape[1]
        pltpu.make_async_copy(xb_ref.at[pl.ds(0, rows)], stage.at[step % 2], sem.at[step % 2]).wait()

    @pl.when(i > 0)
    def _():
        wait_all(i - 1)

    @pl.when(i == last)
    def _():
        wait_all(i)


def _dispatch(zslot, tslot, tcnt, toff, h, lpos_flat, n_slots):
    t = h.shape[0] // SUBLANES
    tm = TOKEN_TILE
    return pl.pallas_call(
        _dispatch_kernel,
        grid_spec=pltpu.PrefetchScalarGridSpec(
            num_scalar_prefetch=4,
            grid=(t // tm,),
            in_specs=[pl.BlockSpec((tm * SUBLANES, LANES), lambda i, *_: (i, 0)),
                      pl.BlockSpec((tm * TOP_K,), lambda i, *_: (i,), memory_space=pltpu.SMEM)],
            out_specs=pl.BlockSpec(memory_space=pl.ANY),
            scratch_shapes=[pltpu.VMEM((SLOT_BLOCK * SUBLANES, LANES), F32),
                            pltpu.VMEM((2, tm * TOP_K * SUBLANES, LANES), F32),
                            pltpu.SemaphoreType.DMA((2,)), pltpu.SemaphoreType.DMA(())],
        ),
        out_shape=jax.ShapeDtypeStruct((n_slots * SUBLANES, LANES), F32),
        compiler_params=_params(1),
        name="dispatch",
    )(zslot, tslot, tcnt, toff, h, lpos_flat)


def _experts_kernel(be_ref, nu_ref, x_ref, w1_ref, b1_ref, w2_ref, b2_ref, y_ref, w1b, w2b):
    i = pl.program_id(0)
    f = w2b.shape[0]
    used = i < nu_ref[0]
    changed = (i == 0) | (be_ref[i] != be_ref[jnp.maximum(i - 1, 0)])

    @pl.when(used & changed)
    def _():
        w1b[...] = w1_ref[0, 0].astype(BF16)
        w2b[...] = w2_ref[0, 0].astype(BF16)

    @pl.when(jnp.logical_not(used))
    def _():
        y_ref[...] = jnp.zeros(y_ref.shape, F32)

    @pl.when(used)
    def _():
        x = _load_token_major(x_ref, 0, SLOT_BLOCK).astype(BF16)
        gu = jnp.dot(x, w1b[...], preferred_element_type=F32) + b1_ref[0, 0]
        g = jnp.minimum(gu[:, :f], SWIGLU_LIMIT)
        u = jnp.clip(gu[:, f:], -SWIGLU_LIMIT, SWIGLU_LIMIT)
        act = (u + 1.0) * (g * _sigmoid(SWIGLU_ALPHA * g))
        y = jnp.dot(act.astype(BF16), w2b[...], preferred_element_type=F32) + b2_ref[0, 0]
        _store_token_major(y_ref, 0, y)


def _experts(layer, block_e, n_used, xb, w1, b1, w2, b2):
    n_slots = xb.shape[0] // SUBLANES
    _, ne, d, f2 = w1.shape
    f = w2.shape[2]
    assert d == TILE_CH
    nb = n_slots // SLOT_BLOCK
    blk = lambda i, be, nu: jnp.minimum(i, nu[0] - 1)
    rows = (SLOT_BLOCK * SUBLANES, LANES)
    per_e = lambda shape: pl.BlockSpec((1, 1) + shape, lambda i, be, nu: (layer, be[blk(i, be, nu)], 0, 0))
    return pl.pallas_call(
        _experts_kernel,
        grid_spec=pltpu.PrefetchScalarGridSpec(
            num_scalar_prefetch=2,
            grid=(nb,),
            in_specs=[pl.BlockSpec(rows, lambda i, be, nu: (blk(i, be, nu), 0)),
                      per_e((d, f2)), per_e((1, f2)), per_e((f, d)), per_e((1, d))],
            out_specs=pl.BlockSpec(rows, lambda i, be, nu: (i, 0)),
            scratch_shapes=[pltpu.VMEM((d, f2), BF16), pltpu.VMEM((f, d), BF16)],
        ),
        out_shape=jax.ShapeDtypeStruct((n_slots * SUBLANES, LANES), F32),
        compiler_params=_params(1),
        name="experts",
    )(block_e, n_used, xb, w1, b1.reshape(b1.shape[0], ne, 1, f2), w2, b2.reshape(b2.shape[0], ne, 1, d))


def _combine_kernel(final_norm, tslot_ref, tcnt_ref, toff_ref, x_ref, lpos_ref, gate_ref, g2_ref, fg_ref, ys_ref,
                    o_ref, stage, obuf, sem):
    tm = x_ref.shape[0]
    ne = N_EXPERTS
    i = pl.program_id(0)
    last = pl.num_programs(0) - 1

    def runs(step, fn):
        dst, s = stage.at[step % 2], sem.at[step % 2]
        _tile_runs(step, ne, tcnt_ref,
                   lambda j, n: _run_copies(n, ys_ref, tslot_ref[j], dst, toff_ref[j], s, fn))

    @pl.when(i == 0)
    def _():
        runs(i, lambda cp: cp.start())

    @pl.when(i < last)
    def _():
        runs(i + 1, lambda cp: cp.start())

    pltpu.make_async_copy(ys_ref.at[pl.ds(0, stage.shape[1])], stage.at[i % 2], sem.at[i % 2]).wait()
    buf = stage.at[i % 2]

    def mix(t, carry):
        acc = gate_ref[t * TOP_K] * buf[_tile_rows(lpos_ref[t * TOP_K]), :]
        for k in range(1, TOP_K):
            acc = acc + gate_ref[t * TOP_K + k] * buf[_tile_rows(lpos_ref[t * TOP_K + k]), :]
        obuf[_tile_rows(t), :] = acc
        return carry

    lax.fori_loop(0, tm, mix, 0, unroll=4)
    x = x_ref[...] + g2_ref[0] * _load_token_major(obuf, 0, tm)
    if final_norm:
        x = x * lax.rsqrt(jnp.mean(x * x, axis=-1, keepdims=True) + EPS) * fg_ref[...]
    o_ref[...] = x


def _combine(tslot, tcnt, toff, x2, lpos_flat, gate_flat, g2, final_g, ys, tokens_per_batch, final_norm):
    t, d = x2.shape
    tm = TOKEN_TILE
    tiles_per_batch = tokens_per_batch // tm
    pairs = pl.BlockSpec((tm * TOP_K,), lambda i, *_: (i,), memory_space=pltpu.SMEM)
    return pl.pallas_call(
        functools.partial(_combine_kernel, final_norm),
        grid_spec=pltpu.PrefetchScalarGridSpec(
            num_scalar_prefetch=3,
            grid=(t // tm,),
            in_specs=[pl.BlockSpec((tm, d), lambda i, *_: (i, 0)), pairs, pairs,
                      pl.BlockSpec((1, 1, d), lambda i, *_: (i // tiles_per_batch, 0, 0)),
                      pl.BlockSpec((1, d), lambda i, *_: (0, 0)),
                      pl.BlockSpec(memory_space=pl.ANY)],
            out_specs=pl.BlockSpec((tm, d), lambda i, *_: (i, 0)),
            scratch_shapes=[pltpu.VMEM((2, tm * TOP_K * SUBLANES, LANES), F32), pltpu.VMEM((tm * SUBLANES, LANES), F32),
                            pltpu.SemaphoreType.DMA((2,))],
        ),
        out_shape=jax.ShapeDtypeStruct((t, d), F32),
        compiler_params=_params(1),
        name="combine",
    )(tslot, tcnt, toff, x2, lpos_flat, gate_flat, g2, final_g.reshape(1, d), ys)


def _moe(layer, x, shift, scale, gate2, norm_g, router_w, router_b, w1, b1, w2, b2, final_g, final_norm):
    bsz, s, d = x.shape
    t = bsz * s
    ne = router_w.shape[1]
    x2 = x.reshape(t, d)
    h, gate, lpos, info, counts = _route(x2, shift, scale, norm_g, router_w, router_b, s)
    counts = counts[:, 0]
    padded = ((counts + SLOT_BLOCK - 1) // SLOT_BLOCK) * SLOT_BLOCK
    ends = jnp.cumsum(padded)
    pstart = (ends - padded).astype(I32)
    n_slots = t * TOP_K + ne * SLOT_BLOCK
    nb = n_slots // SLOT_BLOCK
    block_start = jnp.arange(nb, dtype=I32) * SLOT_BLOCK
    block_e = jnp.minimum(jnp.sum(block_start[:, None] >= ends[None, :], axis=1), ne - 1).astype(I32)
    n_used = (ends[-1:] // SLOT_BLOCK).astype(I32)
    trailing = n_used[0] + jnp.arange(nb - t * TOP_K // SLOT_BLOCK, dtype=I32)
    zslot = jnp.concatenate([jnp.where(padded > 0, ends - SLOT_BLOCK, -1),
                             jnp.where(trailing < nb, trailing * SLOT_BLOCK, -1)]).astype(I32)
    tcnt, tbase, toff = info[:, 0], info[:, 1], info[:, 2]
    tslot = tbase + jnp.tile(pstart, t // TOKEN_TILE)
    pair_major = lambda a: a.T.reshape(t * TOP_K)
    lpos_flat = pair_major(lpos)
    xb = _dispatch(zslot, tslot, tcnt, toff, h, lpos_flat, n_slots)
    ys = _experts(layer, block_e, n_used, xb, w1, b1, w2, b2)
    out = _combine(tslot, tcnt, toff, x2, lpos_flat, pair_major(gate), gate2, final_g, ys, s, final_norm)
    return out.reshape(bsz, s, d)


def kernel(x, c, ada_w, ada_b, mix_norm_g, ffn_norm_g, conf_pw1_w, conf_pw1_b, conf_dw_w, conf_dw_b, conf_ln_g, conf_ln_b, conf_pw2_w, conf_pw2_b, ssm_in_w, ssm_conv_w, ssm_conv_b, ssm_dt_bias, ssm_a_log, ssm_d, ssm_norm_g, ssm_out_w, router_w, router_b, exp_w1, exp_b1, exp_w2, exp_b2, final_norm_g):
    depth = ada_w.shape[0]
    bsz, s, d = x.shape
    mod = _ada(c, ada_w, ada_b).reshape(depth, bsz, 6, 1, d)
    for i in range(depth):
        sh1, sc1, g1, sh2, sc2, g2 = (mod[i, :, m] for m in range(6))
        j = i // 2
        if i % 2 == 0:
            x = _conformer(x, sh1, sc1, g1, mix_norm_g[i], conf_pw1_w[j], conf_pw1_b[j], conf_dw_w[j], conf_dw_b[j],
                           conf_ln_g[j], conf_ln_b[j], conf_pw2_w[j], conf_pw2_b[j])
        else:
            z, xs, bm, cm, dt = _ssm_in(x, sh1, sc1, mix_norm_g[i], ssm_in_w[j], ssm_conv_w[j], ssm_conv_b[j],
                                        ssm_dt_bias[j])
            y = _ssd(xs, bm, cm, dt, z, ssm_a_log[j], ssm_d[j], ssm_norm_g[j])
            x = _out_proj(y, x, g1, ssm_out_w[j])
        x = _moe(i, x, sh2, sc2, g2, ffn_norm_g[i], router_w[i], router_b[i], exp_w1, exp_b1, exp_w2, exp_b2,
                 final_norm_g, final_norm=(i == depth - 1))
    return x
```

```python
import functools

import jax
import jax.numpy as jnp
from jax import lax
from jax.experimental import pallas as pl
from jax.experimental.pallas import tpu as pltpu

F32 = jnp.float32
BF16 = jnp.bfloat16
I32 = jnp.int32

EPS = 1e-5
CONV_WIDTH = 31
SSM_HEAD_DIM = 64
SSM_GROUPS = 8
SSM_STATE = 128
SSM_CONV = 4
SSD_CHUNK = 128
N_EXPERTS = 32
TOP_K = 4
SWIGLU_LIMIT = 7.0
SWIGLU_ALPHA = 1.702

LANES = 128
SUBLANES = 8
TILE_CH = LANES * SUBLANES
SLOT_BLOCK = 512
TOKEN_TILE = 512
SEQ_TILE = 256
HALO = 32
SSM_HALO = 8
CONV_GROUP = 8
VMEM_LIMIT = 56 * 1024 * 1024

HIGHEST = lax.Precision.HIGHEST


def _params(n_axes, vmem=VMEM_LIMIT):
    return pltpu.CompilerParams(dimension_semantics=("arbitrary",) * n_axes, vmem_limit_bytes=vmem)


def _sigmoid(v):
    return 1.0 / (1.0 + jnp.exp(-v))


def _silu(v):
    return v * _sigmoid(v)


def _rms_modulate(x, g, shift, scale):
    ms = jnp.mean(x * x, axis=-1, keepdims=True)
    return (x * lax.rsqrt(ms + EPS) * g) * (1.0 + scale) + shift


def _split_bf16(v):
    hi = v.astype(BF16)
    lo = (v - hi.astype(F32)).astype(BF16)
    return hi, lo


def _store_token_major(ref, tok0, v):
    n, d = v.shape
    assert d == TILE_CH and n % SUBLANES == 0
    for th in range(n // SUBLANES):
        for j in range(SUBLANES):
            ref[pl.ds((tok0 + th * SUBLANES) * SUBLANES + j, SUBLANES, stride=SUBLANES), :] = (
                v[th * SUBLANES:(th + 1) * SUBLANES, j * LANES:(j + 1) * LANES])


def _load_token_major(ref, tok0, n):
    rows = []
    for th in range(n // SUBLANES):
        rows.append(jnp.concatenate(
            [ref[pl.ds((tok0 + th * SUBLANES) * SUBLANES + j, SUBLANES, stride=SUBLANES), :] for j in range(SUBLANES)],
            axis=1))
    return jnp.concatenate(rows, axis=0)


def _tile_rows(tok):
    if isinstance(tok, int):
        return pl.ds(tok * SUBLANES, SUBLANES)
    return pl.ds(pl.multiple_of(tok * SUBLANES, SUBLANES), SUBLANES)


def _token_major_conv(src, dst, w_ref, b_tile, n_tokens, width, halo, post, unrolled=False):
    base = halo - (width - 1)

    def group(gi, carry):
        t0 = gi * CONV_GROUP
        tiles = [src[_tile_rows(t0 + base + i), :] for i in range(width + CONV_GROUP - 1)]
        for o in range(CONV_GROUP):
            acc = [b_tile, None]
            for k in range(width):
                term = w_ref[k * SUBLANES:(k + 1) * SUBLANES, :] * tiles[o + k]
                acc[k % 2] = term if acc[k % 2] is None else acc[k % 2] + term
            dst[_tile_rows(t0 + o), :] = post(acc[0] + acc[1])
        return carry

    if unrolled:
        for gi in range(n_tokens // CONV_GROUP):
            group(gi, 0)
    else:
        lax.fori_loop(0, n_tokens // CONV_GROUP, group, 0)


def _ada_kernel(c_ref, w_ref, b_ref, o_ref):
    c = c_ref[...]
    o_ref[0] = jnp.dot(_silu(c), w_ref[0], precision=HIGHEST, preferred_element_type=F32) + b_ref[0]


def _ada(c, ada_w, ada_b):
    depth, d, n = ada_w.shape
    bsz = c.shape[0]
    tn = 1536
    return pl.pallas_call(
        _ada_kernel,
        grid=(depth, n // tn),
        in_specs=[pl.BlockSpec((bsz, d), lambda l, j: (0, 0)),
                  pl.BlockSpec((1, d, tn), lambda l, j: (l, 0, j)),
                  pl.BlockSpec((1, 1, tn), lambda l, j: (l, 0, j))],
        out_specs=pl.BlockSpec((1, bsz, tn), lambda l, j: (l, 0, j)),
        out_shape=jax.ShapeDtypeStruct((depth, bsz, n), F32),
        compiler_params=_params(2),
        name="ada",
    )(c, ada_w, ada_b.reshape(depth, 1, n))


def _conformer_kernel(x_ref, sh_ref, sc_ref, gate_ref, ng_ref, w1_ref, b1_ref, dw_ref, dwb_ref,
                      lng_ref, lnb_ref, w2_ref, b2_ref, o_ref, ubuf, cbuf):
    ts, d = x_ref.shape[1], x_ref.shape[2]
    x = x_ref[0]
    h = _rms_modulate(x, ng_ref[...], sh_ref[0], sc_ref[0])
    u = jnp.dot(h.astype(BF16), w1_ref[...], preferred_element_type=F32) + b1_ref[...]
    u = u[:, :d] * _sigmoid(u[:, d:])

    @pl.when(pl.program_id(1) == 0)
    def _():
        ubuf[0:HALO * SUBLANES, :] = jnp.zeros((HALO * SUBLANES, LANES), F32)

    _store_token_major(ubuf, HALO, u)
    _token_major_conv(ubuf, cbuf, dw_ref, dwb_ref[...], ts, CONV_WIDTH, HALO, lambda a: a, unrolled=True)
    ubuf[0:HALO * SUBLANES, :] = ubuf[ts * SUBLANES:(ts + HALO) * SUBLANES, :]

    v = _load_token_major(cbuf, 0, ts)
    mu = jnp.mean(v, axis=-1, keepdims=True)
    vc = v - mu
    var = jnp.mean(vc * vc, axis=-1, keepdims=True)
    v = _silu(vc * lax.rsqrt(var + EPS) * lng_ref[...] + lnb_ref[...])
    y = jnp.dot(v.astype(BF16), w2_ref[...], preferred_element_type=F32) + b2_ref[...]
    o_ref[0] = x + gate_ref[0] * y


def _conformer(x, shift, scale, gate, norm_g, pw1_w, pw1_b, dw_w, dw_b, ln_g, ln_b, pw2_w, pw2_b):
    bsz, s, d = x.shape
    assert d == TILE_CH
    ts = SEQ_TILE
    row = lambda a: a.reshape(1, -1)
    per_batch = pl.BlockSpec((1, 1, d), lambda b, j: (b, 0, 0))
    full = lambda a: pl.BlockSpec(a.shape, lambda b, j: (0,) * a.ndim)
    consts = [row(norm_g), pw1_w.astype(BF16), row(pw1_b), dw_w.reshape(CONV_WIDTH * SUBLANES, LANES),
              dw_b.reshape(SUBLANES, LANES), row(ln_g), row(ln_b), pw2_w.astype(BF16), row(pw2_b)]
    return pl.pallas_call(
        _conformer_kernel,
        grid=(bsz, s // ts),
        in_specs=[pl.BlockSpec((1, ts, d), lambda b, j: (b, j, 0)), per_batch, per_batch, per_batch]
                 + [full(a) for a in consts],
        out_specs=pl.BlockSpec((1, ts, d), lambda b, j: (b, j, 0)),
        out_shape=jax.ShapeDtypeStruct((bsz, s, d), F32),
        scratch_shapes=[pltpu.VMEM(((HALO + ts) * SUBLANES, LANES), F32), pltpu.VMEM((ts * SUBLANES, LANES), F32)],
        compiler_params=_params(2),
        name="conformer",
    )(x, shift, scale, gate, *consts)


def _ssm_in_kernel(x_ref, sh_ref, sc_ref, ng_ref, wz_ref, wx_ref, wdt_ref, cw_ref, cb_ref, dtb_ref,
                   z_ref, xs_ref, b_ref, c_ref, dt_ref, buf, cbuf):
    ts = x_ref.shape[1]
    d_inner = xs_ref.shape[2]
    gn = b_ref.shape[2]
    n_col = buf.shape[0]
    nh = d_inner // SSM_HEAD_DIM
    h = _rms_modulate(x_ref[0], ng_ref[...], sh_ref[0], sc_ref[0]).astype(BF16)
    z_ref[0] = jnp.dot(h, wz_ref[...], preferred_element_type=F32).astype(BF16)
    dt = jnp.dot(h, wdt_ref[...], preferred_element_type=F32) + dtb_ref[...]
    dt = jnp.maximum(dt, 0.0) + jnp.log(1.0 + jnp.exp(-jnp.abs(dt)))
    dt_ref[0] = jnp.where(lax.broadcasted_iota(I32, dt.shape, 1) < nh, dt, 0.0)

    @pl.when(pl.program_id(1) == 0)
    def _():
        buf[:, 0:SSM_HALO * SUBLANES, :] = jnp.zeros((n_col, SSM_HALO * SUBLANES, LANES), F32)

    outs = []
    for q in range(d_inner // TILE_CH):
        outs.append((xs_ref, q * TILE_CH))
    for q in range(gn // TILE_CH):
        outs.append((b_ref, q * TILE_CH))
    for q in range(gn // TILE_CH):
        outs.append((c_ref, q * TILE_CH))
    for q in range(n_col):
        xq = jnp.dot(h, wx_ref[:, q * TILE_CH:(q + 1) * TILE_CH], preferred_element_type=F32)
        _store_token_major(buf.at[q], SSM_HALO, xq)
        _token_major_conv(buf.at[q], cbuf.at[q], cw_ref.at[q], cb_ref[q], ts, SSM_CONV, SSM_HALO, _silu,
                          unrolled=True)
        buf[q, 0:SSM_HALO * SUBLANES, :] = buf[q, ts * SUBLANES:(ts + SSM_HALO) * SUBLANES, :]
        ref, lo = outs[q]
        ref[0, :, lo:lo + TILE_CH] = _load_token_major(cbuf.at[q], 0, ts).astype(BF16)


def _ssm_in(x, shift, scale, norm_g, in_w, conv_w, conv_b, dt_bias):
    bsz, s, d = x.shape
    ts = SEQ_TILE
    nh = dt_bias.shape[0]
    d_inner = nh * SSM_HEAD_DIM
    gn = SSM_GROUPS * SSM_STATE
    conv_dim = d_inner + 2 * gn
    n_col = conv_dim // TILE_CH
    assert d_inner % TILE_CH == 0 and gn % TILE_CH == 0
    w = in_w.astype(BF16)
    lane_pad = lambda a: jnp.pad(a, ((0, 0), (0, LANES - nh)))
    cw = conv_w.reshape(SSM_CONV, n_col, SUBLANES, LANES).transpose(1, 0, 2, 3).reshape(n_col, SSM_CONV * SUBLANES, LANES)
    cb = conv_b.reshape(n_col, SUBLANES, LANES)
    consts = [norm_g.reshape(1, d), w[:, :d_inner], w[:, d_inner:d_inner + conv_dim],
              lane_pad(w[:, d_inner + conv_dim:]), cw, cb, lane_pad(dt_bias.reshape(1, nh))]
    per_batch = pl.BlockSpec((1, 1, d), lambda b, j: (b, 0, 0))
    full = lambda a: pl.BlockSpec(a.shape, lambda b, j: (0,) * a.ndim)
    tile = lambda n: pl.BlockSpec((1, ts, n), lambda b, j: (b, j, 0))
    return pl.pallas_call(
        _ssm_in_kernel,
        grid=(bsz, s // ts),
        in_specs=[tile(d), per_batch, per_batch] + [full(a) for a in consts],
        out_specs=[tile(d_inner), tile(d_inner), tile(gn), tile(gn), tile(LANES)],
        out_shape=[jax.ShapeDtypeStruct((bsz, s, d_inner), BF16), jax.ShapeDtypeStruct((bsz, s, d_inner), BF16),
                   jax.ShapeDtypeStruct((bsz, s, gn), BF16), jax.ShapeDtypeStruct((bsz, s, gn), BF16),
                   jax.ShapeDtypeStruct((bsz, s, LANES), F32)],
        scratch_shapes=[pltpu.VMEM((n_col, (SSM_HALO + ts) * SUBLANES, LANES), F32),
                        pltpu.VMEM((n_col, ts * SUBLANES, LANES), F32)],
        compiler_params=_params(2),
        name="ssm_in",
    )(x, shift, scale, *consts)


def _expand_heads(v, rep):
    hi, lo = _split_bf16(v)
    return (jnp.dot(hi, rep, preferred_element_type=F32) + jnp.dot(lo, rep, preferred_element_type=F32))


def _ssd_kernel(xs_ref, b_ref, c_ref, dt_ref, z_ref, alog_ref, dskip_ref, ng_ref, o_ref, state):
    L = xs_ref.shape[1]
    d_inner = xs_ref.shape[2]
    hp = dt_ref.shape[2]
    hg = d_inner // SSM_HEAD_DIM // SSM_GROUPS
    gw = hg * SSM_HEAD_DIM
    n = SSM_STATE

    @pl.when(pl.program_id(1) == 0)
    def _():
        state[...] = jnp.zeros(state.shape, F32)

    dt = dt_ref[0]
    a = dt * (-jnp.exp(alog_ref[...]))
    row = lax.broadcasted_iota(I32, (L, L), 0)
    col = lax.broadcasted_iota(I32, (L, L), 1)
    causal = row >= col
    tri = causal.astype(F32)
    cum = jnp.dot(tri, a, precision=HIGHEST, preferred_element_type=F32)
    cum_t = cum.T
    dt_t = dt.T
    total = cum[L - 1:L, :]

    head_of = lax.broadcasted_iota(I32, (hp, d_inner), 1) // SSM_HEAD_DIM
    rep = (head_of == lax.broadcasted_iota(I32, (hp, d_inner), 0)).astype(BF16)
    grow = _expand_heads(jnp.exp(cum), rep)
    to_end = _expand_heads(jnp.exp(total - cum) * dt, rep)
    keep = _expand_heads(jnp.exp(total), rep)
    head_in_group = lax.broadcasted_iota(I32, (L, gw), 1) // SSM_HEAD_DIM

    xs = xs_ref[0]
    xf = xs.astype(F32)
    for g in range(SSM_GROUPS):
        bg = b_ref[0, :, g * n:(g + 1) * n]
        cg = c_ref[0, :, g * n:(g + 1) * n]
        xg = xs[:, g * gw:(g + 1) * gw]
        cb = lax.dot_general(cg, bg, (((1,), (1,)), ((), ())), preferred_element_type=F32)
        st = state[g]
        y_off = jnp.dot(cg, st.astype(BF16), preferred_element_type=F32)
        y_diag = jnp.zeros((L, gw), F32)
        for hh in range(hg):
            h = g * hg + hh
            seg = cum[:, h:h + 1] - cum_t[h:h + 1, :]
            w = jnp.where(causal, jnp.exp(seg), 0.0) * cb * dt_t[h:h + 1, :]
            x_head = jnp.where(head_in_group == hh, xg, jnp.zeros_like(xg))
            y_diag = y_diag + jnp.dot(w.astype(BF16), x_head, preferred_element_type=F32)
        sl = slice(g * gw, (g + 1) * gw)
        y = y_diag + y_off * grow[:, sl] + dskip_ref[:, sl] * xf[:, sl]
        xw = (xf[:, sl] * to_end[:, sl]).astype(BF16)
        upd = lax.dot_general(bg, xw, (((0,), (0,)), ((), ())), preferred_element_type=F32)
        state[g] = st * keep[:, sl] + upd
        y = y * _silu(z_ref[0, :, sl].astype(F32))
        y = y * lax.rsqrt(jnp.mean(y * y, axis=-1, keepdims=True) + EPS)
        o_ref[0, :, sl] = (y * ng_ref[:, sl]).astype(BF16)


def _ssd(xs, bm, cm, dt, z, a_log, d_skip, norm_g):
    bsz, s, d_inner = xs.shape
    nh = a_log.shape[0]
    hp = dt.shape[2]
    gn = bm.shape[2]
    L = SSD_CHUNK
    gw = d_inner // SSM_GROUPS
    consts = [jnp.pad(a_log.reshape(1, nh), ((0, 0), (0, hp - nh))),
              jnp.repeat(d_skip, SSM_HEAD_DIM).reshape(1, d_inner), norm_g.reshape(1, d_inner)]
    tile = lambda n: pl.BlockSpec((1, L, n), lambda b, j: (b, j, 0))
    full = lambda a: pl.BlockSpec(a.shape, lambda b, j: (0,) * a.ndim)
    return pl.pallas_call(
        _ssd_kernel,
        grid=(bsz, s // L),
        in_specs=[tile(d_inner), tile(gn), tile(gn), tile(hp), tile(d_inner)] + [full(a) for a in consts],
        out_specs=tile(d_inner),
        out_shape=jax.ShapeDtypeStruct((bsz, s, d_inner), BF16),
        scratch_shapes=[pltpu.VMEM((SSM_GROUPS, SSM_STATE, gw), F32)],
        compiler_params=_params(2),
        name="ssd",
    )(xs, bm, cm, dt, z, *consts)


def _out_proj_kernel(y_ref, x_ref, gate_ref, w_ref, o_ref):
    o_ref[0] = x_ref[0] + gate_ref[0] * jnp.dot(y_ref[0], w_ref[...], preferred_element_type=F32)


def _out_proj(y, x, gate, out_w):
    bsz, s, d = x.shape
    k = y.shape[2]
    ts = 512
    return pl.pallas_call(
        _out_proj_kernel,
        grid=(bsz, s // ts),
        in_specs=[pl.BlockSpec((1, ts, k), lambda b, j: (b, j, 0)), pl.BlockSpec((1, ts, d), lambda b, j: (b, j, 0)),
                  pl.BlockSpec((1, 1, d), lambda b, j: (b, 0, 0)), pl.BlockSpec((k, d), lambda b, j: (0, 0))],
        out_specs=pl.BlockSpec((1, ts, d), lambda b, j: (b, j, 0)),
        out_shape=jax.ShapeDtypeStruct((bsz, s, d), F32),
        compiler_params=_params(2),
        name="out_proj",
    )(y, x, gate, out_w.astype(BF16))


def _route_kernel(x_ref, sh_ref, sc_ref, ng_ref, wt_ref, rb_ref, h_ref, gate_ref, lpos_ref, info_ref, cnt_ref, running):
    tm = x_ref.shape[0]
    ne = wt_ref.shape[0]

    @pl.when(pl.program_id(0) == 0)
    def _():
        running[...] = jnp.zeros(running.shape, F32)

    h = _rms_modulate(x_ref[...], ng_ref[...], sh_ref[0], sc_ref[0])
    _store_token_major(h_ref, 0, h)
    h_hi, h_lo = _split_bf16(h)
    w_hi, w_lo = _split_bf16(wt_ref[...])
    nt = (((1,), (1,)), ((), ()))
    logits = (lax.dot_general(w_hi, h_hi, nt, preferred_element_type=F32)
              + lax.dot_general(w_hi, h_lo, nt, preferred_element_type=F32)
              + lax.dot_general(w_lo, h_hi, nt, preferred_element_type=F32)) + rb_ref[...]

    e_iota = lax.broadcasted_iota(I32, (ne, tm), 0).astype(F32)
    vals = logits
    tops, sels = [], []
    for _ in range(TOP_K):
        m = jnp.max(vals, axis=0, keepdims=True)
        idx = jnp.min(jnp.where(vals == m, e_iota, float(ne)), axis=0, keepdims=True)
        sel = e_iota == idx
        vals = jnp.where(sel, -jnp.inf, vals)
        tops.append(m)
        sels.append(sel)
    exps = [jnp.exp(t - tops[0]) for t in tops]
    inv = 1.0 / (exps[0] + exps[1] + exps[2] + exps[3])

    member = jnp.zeros((ne, tm), F32)
    for sel in sels:
        member = member + sel.astype(F32)
    before = (lax.broadcasted_iota(I32, (tm, tm), 0) < lax.broadcasted_iota(I32, (tm, tm), 1)).astype(BF16)
    earlier = jnp.dot(member.astype(BF16), before, preferred_element_type=F32)
    cnt = jnp.sum(member, axis=1, keepdims=True)
    lower = (lax.broadcasted_iota(I32, (ne, ne), 0) > lax.broadcasted_iota(I32, (ne, ne), 1)).astype(BF16)
    cnt_hi, cnt_lo = _split_bf16(jnp.broadcast_to(cnt, (ne, LANES)))
    off = (jnp.dot(lower, cnt_hi, preferred_element_type=F32)
           + jnp.dot(lower, cnt_lo, preferred_element_type=F32))[:, 0:1]
    for k in range(TOP_K):
        gate_ref[k:k + 1, :] = exps[k] * inv
        lpos_ref[k:k + 1, :] = jnp.sum(jnp.where(sels[k], earlier + off, 0.0), axis=0, keepdims=True).astype(I32)
    lane = lax.broadcasted_iota(I32, (ne, LANES), 1)
    info = jnp.where(lane == 0, cnt, jnp.where(lane == 1, running[...], jnp.where(lane == 2, off, 0.0)))
    info_ref[...] = info.astype(I32)
    running[...] = running[...] + cnt
    cnt_ref[...] = jnp.broadcast_to(running[...], cnt_ref.shape).astype(I32)


def _route(x2, shift, scale, norm_g, router_w, router_b, tokens_per_batch):
    t, d = x2.shape
    assert d == TILE_CH
    ne = router_w.shape[1]
    tm = TOKEN_TILE
    tiles_per_batch = tokens_per_batch // tm
    per_batch = pl.BlockSpec((1, 1, d), lambda i: (i // tiles_per_batch, 0, 0))
    full = lambda shape: pl.BlockSpec(shape, lambda i: (0,) * len(shape))
    kt = pl.BlockSpec((TOP_K, tm), lambda i: (0, i))
    return pl.pallas_call(
        _route_kernel,
        grid=(t // tm,),
        in_specs=[pl.BlockSpec((tm, d), lambda i: (i, 0)), per_batch, per_batch,
                  full((1, d)), full((ne, d)), full((ne, 1))],
        out_specs=[pl.BlockSpec((tm * SUBLANES, LANES), lambda i: (i, 0)), kt, kt,
                   pl.BlockSpec((ne, LANES), lambda i: (i, 0)), full((ne, LANES))],
        out_shape=[jax.ShapeDtypeStruct((t * SUBLANES, LANES), F32), jax.ShapeDtypeStruct((TOP_K, t), F32),
                   jax.ShapeDtypeStruct((TOP_K, t), I32), jax.ShapeDtypeStruct((t // tm * ne, LANES), I32),
                   jax.ShapeDtypeStruct((ne, LANES), I32)],
        scratch_shapes=[pltpu.VMEM((ne, 1), F32)],
        compiler_params=_params(1),
        name="route",
    )(x2, shift, scale, norm_g.reshape(1, d), router_w.T, router_b.reshape(ne, 1))


RUN_PIECES = tuple(1 << b for b in range(TOKEN_TILE.bit_length() - 1, -1, -1))


def _run_copies(n, src_ref, src_tok, dst_ref, dst_tok, sem, fn):
    for b in RUN_PIECES:
        done = n & ~(2 * b - 1)

        @pl.when((n & b) != 0)
        def _():
            src = pl.ds(pl.multiple_of((src_tok + done) * SUBLANES, SUBLANES), b * SUBLANES)
            dst = pl.ds(pl.multiple_of((dst_tok + done) * SUBLANES, SUBLANES), b * SUBLANES)
            fn(pltpu.make_async_copy(src_ref.at[src], dst_ref.at[dst], sem))


def _tile_runs(step, n_experts, cnt_ref, fn):
    def body(e, carry):
        j = step * n_experts + e
        fn(j, cnt_ref[j])
        return carry

    lax.fori_loop(0, n_experts, body, 0)


def _dispatch_kernel(zslot_ref, tslot_ref, tcnt_ref, toff_ref, h_ref, lpos_ref, xb_ref, zeros, stage, sem, zsem):
    tm = h_ref.shape[0] // SUBLANES
    block_rows = SLOT_BLOCK * SUBLANES
    ne = N_EXPERTS
    i = pl.program_id(0)
    last = pl.num_programs(0) - 1

    @pl.when(pl.program_id(0) == 0)
    def _():
        zeros[...] = jnp.zeros(zeros.shape, F32)

        def zcopy(e):
            rows = pl.ds(pl.multiple_of(zslot_ref[e] * SUBLANES, block_rows), block_rows)
            return pltpu.make_async_copy(zeros, xb_ref.at[rows], zsem)

        def zstart(e, carry):
            @pl.when(zslot_ref[e] >= 0)
            def _():
                zcopy(e).start()
            return carry

        def zwait(e, carry):
            @pl.when(zslot_ref[e] >= 0)
            def _():
                zcopy(e).wait()
            return carry

        lax.fori_loop(0, zslot_ref.shape[0], zstart, 0)
        lax.fori_loop(0, zslot_ref.shape[0], zwait, 0)

    buf = stage.at[i % 2]

    def place(t, carry):
        tile = h_ref[_tile_rows(t), :]
        for k in range(TOP_K):
            buf[_tile_rows(lpos_ref[t * TOP_K + k]), :] = tile
        return carry

    lax.fori_loop(0, tm, place, 0, unroll=4)

    def runs(step, fn):
        src, s = stage.at[step % 2], sem.at[step % 2]
        _tile_runs(step, ne, tcnt_ref,
                   lambda j, n: _run_copies(n, src, toff_ref[j], xb_ref, tslot_ref[j], s, fn))

    runs(i, lambda cp: cp.start())

    def wait_all(step):
        rows = stage.shape[1]
        pltpu.make_async_copy(xb_ref.at[pl.ds(0, rows)], stage.at[step % 2], sem.at[step % 2]).wait()

    @pl.when(i > 0)
    def _():
        wait_all(i - 1)

    @pl.when(i == last)
    def _():
        wait_all(i)


def _dispatch(zslot, tslot, tcnt, toff, h, lpos_flat, n_slots):
    t = h.shape[0] // SUBLANES
    tm = TOKEN_TILE
    return pl.pallas_call(
        _dispatch_kernel,
        grid_spec=pltpu.PrefetchScalarGridSpec(
            num_scalar_prefetch=4,
            grid=(t // tm,),
            in_specs=[pl.BlockSpec((tm * SUBLANES, LANES), lambda i, *_: (i, 0)),
                      pl.BlockSpec((tm * TOP_K,), lambda i, *_: (i,), memory_space=pltpu.SMEM)],
            out_specs=pl.BlockSpec(memory_space=pl.ANY),
            scratch_shapes=[pltpu.VMEM((SLOT_BLOCK * SUBLANES, LANES), F32),
                            pltpu.VMEM((2, tm * TOP_K * SUBLANES, LANES), F32),
                            pltpu.SemaphoreType.DMA((2,)), pltpu.SemaphoreType.DMA(())],
        ),
        out_shape=jax.ShapeDtypeStruct((n_slots * SUBLANES, LANES), F32),
        compiler_params=_params(1),
        name="dispatch",
    )(zslot, tslot, tcnt, toff, h, lpos_flat)


def _experts_kernel(be_ref, nu_ref, x_ref, w1_ref, b1_ref, w2_ref, b2_ref, y_ref, w1b, w2b):
    i = pl.program_id(0)
    f = w2b.shape[0]
    used = i < nu_ref[0]
    changed = (i == 0) | (be_ref[i] != be_ref[jnp.maximum(i - 1, 0)])

    @pl.when(used & changed)
    def _():
        w1b[...] = w1_ref[0, 0].astype(BF16)
        w2b[...] = w2_ref[0, 0].astype(BF16)

    @pl.when(jnp.logical_not(used))
    def _():
        y_ref[...] = jnp.zeros(y_ref.shape, F32)

    @pl.when(used)
    def _():
        x = _load_token_major(x_ref, 0, SLOT_BLOCK).astype(BF16)
        gu = jnp.dot(x, w1b[...], preferred_element_type=F32) + b1_ref[0, 0]
        g = jnp.minimum(gu[:, :f], SWIGLU_LIMIT)
        u = jnp.clip(gu[:, f:], -SWIGLU_LIMIT, SWIGLU_LIMIT)
        act = (u + 1.0) * (g * _sigmoid(SWIGLU_ALPHA * g))
        y = jnp.dot(act.astype(BF16), w2b[...], preferred_element_type=F32) + b2_ref[0, 0]
        _store_token_major(y_ref, 0, y)


def _experts(layer, block_e, n_used, xb, w1, b1, w2, b2):
    n_slots = xb.shape[0] // SUBLANES
    _, ne, d, f2 = w1.shape
    f = w2.shape[2]
    assert d == TILE_CH
    nb = n_slots // SLOT_BLOCK
    blk = lambda i, be, nu: jnp.minimum(i, nu[0] - 1)
    rows = (SLOT_BLOCK * SUBLANES, LANES)
    per_e = lambda shape: pl.BlockSpec((1, 1) + shape, lambda i, be, nu: (layer, be[blk(i, be, nu)], 0, 0))
    return pl.pallas_call(
        _experts_kernel,
        grid_spec=pltpu.PrefetchScalarGridSpec(
            num_scalar_prefetch=2,
            grid=(nb,),
            in_specs=[pl.BlockSpec(rows, lambda i, be, nu: (blk(i, be, nu), 0)),
                      per_e((d, f2)), per_e((1, f2)), per_e((f, d)), per_e((1, d))],
            out_specs=pl.BlockSpec(rows, lambda i, be, nu: (i, 0)),
            scratch_shapes=[pltpu.VMEM((d, f2), BF16), pltpu.VMEM((f, d), BF16)],
        ),
        out_shape=jax.ShapeDtypeStruct((n_slots * SUBLANES, LANES), F32),
        compiler_params=_params(1),
        name="experts",
    )(block_e, n_used, xb, w1, b1.reshape(b1.shape[0], ne, 1, f2), w2, b2.reshape(b2.shape[0], ne, 1, d))


def _combine_kernel(final_norm, tslot_ref, tcnt_ref, toff_ref, x_ref, lpos_ref, gate_ref, g2_ref, fg_ref, ys_ref,
                    o_ref, stage, obuf, sem):
    tm = x_ref.shape[0]
    ne = N_EXPERTS
    i = pl.program_id(0)
    last = pl.num_programs(0) - 1

    def runs(step, fn):
        dst, s = stage.at[step % 2], sem.at[step % 2]
        _tile_runs(step, ne, tcnt_ref,
                   lambda j, n: _run_copies(n, ys_ref, tslot_ref[j], dst, toff_ref[j], s, fn))

    @pl.when(i == 0)
    def _():
        runs(i, lambda cp: cp.start())

    @pl.when(i < last)
    def _():
        runs(i + 1, lambda cp: cp.start())

    pltpu.make_async_copy(ys_ref.at[pl.ds(0, stage.shape[1])], stage.at[i % 2], sem.at[i % 2]).wait()
    buf = stage.at[i % 2]

    def mix(t, carry):
        acc = gate_ref[t * TOP_K] * buf[_tile_rows(lpos_ref[t * TOP_K]), :]
        for k in range(1, TOP_K):
            acc = acc + gate_ref[t * TOP_K + k] * buf[_tile_rows(lpos_ref[t * TOP_K + k]), :]
        obuf[_tile_rows(t), :] = acc
        return carry

    lax.fori_loop(0, tm, mix, 0, unroll=4)
    x = x_ref[...] + g2_ref[0] * _load_token_major(obuf, 0, tm)
    if final_norm:
        x = x * lax.rsqrt(jnp.mean(x * x, axis=-1, keepdims=True) + EPS) * fg_ref[...]
    o_ref[...] = x


def _combine(tslot, tcnt, toff, x2, lpos_flat, gate_flat, g2, final_g, ys, tokens_per_batch, final_norm):
    t, d = x2.shape
    tm = TOKEN_TILE
    tiles_per_batch = tokens_per_batch // tm
    pairs = pl.BlockSpec((tm * TOP_K,), lambda i, *_: (i,), memory_space=pltpu.SMEM)
    return pl.pallas_call(
        functools.partial(_combine_kernel, final_norm),
        grid_spec=pltpu.PrefetchScalarGridSpec(
            num_scalar_prefetch=3,
            grid=(t // tm,),
            in_specs=[pl.BlockSpec((tm, d), lambda i, *_: (i, 0)), pairs, pairs,
                      pl.BlockSpec((1, 1, d), lambda i, *_: (i // tiles_per_batch, 0, 0)),
                      pl.BlockSpec((1, d), lambda i, *_: (0, 0)),
                      pl.BlockSpec(memory_space=pl.ANY)],
            out_specs=pl.BlockSpec((tm, d), lambda i, *_: (i, 0)),
            scratch_shapes=[pltpu.VMEM((2, tm * TOP_K * SUBLANES, LANES), F32), pltpu.VMEM((tm * SUBLANES, LANES), F32),
                            pltpu.SemaphoreType.DMA((2,))],
        ),
        out_shape=jax.ShapeDtypeStruct((t, d), F32),
        compiler_params=_params(1),
        name="combine",
    )(tslot, tcnt, toff, x2, lpos_flat, gate_flat, g2, final_g.reshape(1, d), ys)


def _moe(layer, x, shift, scale, gate2, norm_g, router_w, router_b, w1, b1, w2, b2, final_g, final_norm):
    bsz, s, d = x.shape
    t = bsz * s
    ne = router_w.shape[1]
    x2 = x.reshape(t, d)
    h, gate, lpos, info, counts = _route(x2, shift, scale, norm_g, router_w, router_b, s)
    counts = counts[:, 0]
    padded = ((counts + SLOT_BLOCK - 1) // SLOT_BLOCK) * SLOT_BLOCK
    ends = jnp.cumsum(padded)
    pstart = (ends - padded).astype(I32)
    n_slots = t * TOP_K + ne * SLOT_BLOCK
    nb = n_slots // SLOT_BLOCK
    block_start = jnp.arange(nb, dtype=I32) * SLOT_BLOCK
    block_e = jnp.minimum(jnp.sum(block_start[:, None] >= ends[None, :], axis=1), ne - 1).astype(I32)
    n_used = (ends[-1:] // SLOT_BLOCK).astype(I32)
    trailing = n_used[0] + jnp.arange(nb - t * TOP_K // SLOT_BLOCK, dtype=I32)
    zslot = jnp.concatenate([jnp.where(padded > 0, ends - SLOT_BLOCK, -1),
                             jnp.where(trailing < nb, trailing * SLOT_BLOCK, -1)]).astype(I32)
    tcnt, tbase, toff = info[:, 0], info[:, 1], info[:, 2]
    tslot = tbase + jnp.tile(pstart, t // TOKEN_TILE)
    pair_major = lambda a: a.T.reshape(t * TOP_K)
    lpos_flat = pair_major(lpos)
    xb = _dispatch(zslot, tslot, tcnt, toff, h, lpos_flat, n_slots)
    ys = _experts(layer, block_e, n_used, xb, w1, b1, w2, b2)
    out = _combine(tslot, tcnt, toff, x2, lpos_flat, pair_major(gate), gate2, final_g, ys, s, final_norm)
    return out.reshape(bsz, s, d)


def kernel(x, c, ada_w, ada_b, mix_norm_g, ffn_norm_g, conf_pw1_w, conf_pw1_b, conf_dw_w, conf_dw_b, conf_ln_g, conf_ln_b, conf_pw2_w, conf_pw2_b, ssm_in_w, ssm_conv_w, ssm_conv_b, ssm_dt_bias, ssm_a_log, ssm_d, ssm_norm_g, ssm_out_w, router_w, router_b, exp_w1, exp_b1, exp_w2, exp_b2, final_norm_g):
    depth = ada_w.shape[0]
    bsz, s, d = x.shape
    mod = _ada(c, ada_w, ada_b).reshape(depth, bsz, 6, 1, d)
    for i in range(depth):
        sh1, sc1, g1, sh2, sc2, g2 = (mod[i, :, m] for m in range(6))
        j = i // 2
        if i % 2 == 0:
            x = _conformer(x, sh1, sc1, g1, mix_norm_g[i], conf_pw1_w[j], conf_pw1_b[j], conf_dw_w[j], conf_dw_b[j],
                           conf_ln_g[j], conf_ln_b[j], conf_pw2_w[j], conf_pw2_b[j])
        else:
            z, xs, bm, cm, dt = _ssm_in(x, sh1, sc1, mix_norm_g[i], ssm_in_w[j], ssm_conv_w[j], ssm_conv_b[j],
                                        ssm_dt_bias[j])
            y = _ssd(xs, bm, cm, dt, z, ssm_a_log[j], ssm_d[j], ssm_norm_g[j])
            x = _out_proj(y, x, g1, ssm_out_w[j])
        x = _moe(i, x, sh2, sc2, g2, ffn_norm_g[i], router_w[i], router_b[i], exp_w1, exp_b1, exp_w2, exp_b2,
                 final_norm_g, final_norm=(i == depth - 1))
    return x
```
